```python
import math
import jax, jax.numpy as jnp
from jax import lax
import numpy as np

D_MODEL = 1024
BATCH = 2
SEQ = 8192
DEPTH = 4

CONV_CH = D_MODEL // 2
CONV_KERNEL = 31
HEAD_DIM = 64
N_HEADS = (D_MODEL // 2) // HEAD_DIM
ATTN_WIDTH = N_HEADS * HEAD_DIM
MOBA_BLOCK = 256
MOBA_TOPK = 3
Q_CHUNK = 128
NUM_BUCKETS = 32
MAX_DISTANCE = 2048
D_FF = ((int(8 * D_MODEL / 3) + 255) // 256) * 256
FFN_CONV = 3
EPS = 1e-6
IN_WIDTH = 2 * CONV_CH + 3 * ATTN_WIDTH + 2 * D_MODEL
IN_SPLITS = (CONV_CH, 2 * CONV_CH, 2 * CONV_CH + ATTN_WIDTH, 2 * CONV_CH + 2 * ATTN_WIDTH,
             2 * CONV_CH + 3 * ATTN_WIDTH, 2 * CONV_CH + 3 * ATTN_WIDTH + D_MODEL)

kernel_name = "hybrid_conformer_moba_gated_trunk"


def rmsnorm(x, g):
    xf = x.astype(jnp.float32)
    y = xf * lax.rsqrt(jnp.mean(xf * xf, axis=-1, keepdims=True) + EPS)
    return (y * g.astype(jnp.float32)).astype(x.dtype)


def layernorm(x, g, b):
    xf = x.astype(jnp.float32)
    mu = jnp.mean(xf, axis=-1, keepdims=True)
    var = jnp.mean(jnp.square(xf - mu), axis=-1, keepdims=True)
    y = (xf - mu) * lax.rsqrt(var + EPS)
    return (y * g.astype(jnp.float32) + b.astype(jnp.float32)).astype(x.dtype)


def causal_dwconv(x, w, b):
    width = w.shape[0]
    y = lax.conv_general_dilated(
        x, w[:, None, :].astype(x.dtype), window_strides=(1,), padding=((width - 1, 0),),
        dimension_numbers=("NWC", "WIO", "NWC"), feature_group_count=x.shape[-1])
    return y + b.astype(x.dtype)


def t5_bucket(dist):
    dist = jnp.maximum(dist, 0)
    max_exact = NUM_BUCKETS // 2
    nf = jnp.maximum(dist, 1).astype(jnp.float32)
    large = max_exact + (jnp.log(nf / max_exact) / math.log(MAX_DISTANCE / max_exact)
                         * (NUM_BUCKETS - max_exact)).astype(jnp.int32)
    large = jnp.minimum(large, NUM_BUCKETS - 1)
    return jnp.where(dist < max_exact, dist, large)


def moba_attention(q, k, v, rel_bias):
    bsz, nh, S, hd = q.shape
    nb = -(-S // MOBA_BLOCK)
    s_pad = nb * MOBA_BLOCK
    pad = ((0, 0), (0, 0), (0, s_pad - S), (0, 0))
    kb = jnp.pad(k, pad).reshape(bsz, nh, nb, MOBA_BLOCK, hd)
    vb = jnp.pad(v, pad).reshape(bsz, nh, nb, MOBA_BLOCK, hd)
    k_mean = jnp.mean(kb.astype(jnp.float32), axis=3)
    pos = jnp.arange(S, dtype=jnp.int32)
    own = pos // MOBA_BLOCK
    gate = jnp.einsum("bhsd,bhnd->bhsn", q.astype(jnp.float32), k_mean)
    fully_past = jnp.arange(nb, dtype=jnp.int32)[None, :] < own[:, None]
    gate = jnp.where(fully_past, gate, -jnp.inf)
    k_sel = min(MOBA_TOPK, nb)
    _, top_idx = lax.top_k(gate, k_sel)
    top_idx = top_idx.astype(jnp.int32)
    sel_valid = top_idx < own[:, None]
    own_b = jnp.broadcast_to(own[:, None], (bsz, nh, S, 1))
    blocks = jnp.concatenate([top_idx, own_b], axis=-1)
    valid = jnp.concatenate([sel_valid, jnp.ones((bsz, nh, S, 1), bool)], axis=-1)
    n_sel = k_sel + 1
    scale = HEAD_DIM ** -0.5
    b_ix = jnp.arange(bsz)[:, None, None, None]
    h_ix = jnp.arange(nh)[None, :, None, None]
    h_ix5 = h_ix[..., None]
    in_block = jnp.arange(MOBA_BLOCK, dtype=jnp.int32)

    def one_chunk(ci):
        start = ci * Q_CHUNK
        qc = lax.dynamic_slice_in_dim(q, start, Q_CHUNK, axis=2)
        bc = lax.dynamic_slice_in_dim(blocks, start, Q_CHUNK, axis=2)
        vc = lax.dynamic_slice_in_dim(valid, start, Q_CHUNK, axis=2)
        qpos = start + jnp.arange(Q_CHUNK, dtype=jnp.int32)
        kg = kb[b_ix, h_ix, bc]
        vg = vb[b_ix, h_ix, bc]
        kpos = bc[..., None] * MOBA_BLOCK + in_block
        dist = qpos[:, None, None] - kpos
        logits = jnp.einsum("bhqd,bhqnkd->bhqnk", qc, kg,
                            preferred_element_type=jnp.float32) * scale
        logits = logits + rel_bias[h_ix5, t5_bucket(dist)].astype(jnp.float32)
        mask = vc[..., None] & (dist >= 0)
        logits = jnp.where(mask, logits, -jnp.inf)
        p = jax.nn.softmax(logits.reshape(bsz, nh, Q_CHUNK, n_sel * MOBA_BLOCK), axis=-1)
        p = p.reshape(bsz, nh, Q_CHUNK, n_sel, MOBA_BLOCK).astype(v.dtype)
        return jnp.einsum("bhqnk,bhqnkd->bhqd", p, vg)

    out = lax.map(one_chunk, jnp.arange(S // Q_CHUNK, dtype=jnp.int32))
    return jnp.moveaxis(out, 0, 2).reshape(bsz, nh, S, hd)


def mixer(h, w_in, conv_w, conv_b, conv_ln_g, conv_ln_b, q_norm_g, k_norm_g,
          w_pa, w_pb, w_o, rel_bias):
    bsz, S, _ = h.shape
    z = h @ w_in
    cv, cg, q, k, v, ga, gb = jnp.split(z, IN_SPLITS, axis=-1)
    a = cv * jax.nn.sigmoid(cg)
    a = causal_dwconv(a, conv_w, conv_b)
    a = jax.nn.silu(layernorm(a, conv_ln_g, conv_ln_b))
    ya = a @ w_pa
    q = rmsnorm(q.reshape(bsz, S, N_HEADS, HEAD_DIM), q_norm_g).transpose(0, 2, 1, 3)
    k = rmsnorm(k.reshape(bsz, S, N_HEADS, HEAD_DIM), k_norm_g).transpose(0, 2, 1, 3)
    v = v.reshape(bsz, S, N_HEADS, HEAD_DIM).transpose(0, 2, 1, 3)
    o = moba_attention(q, k, v, rel_bias)
    yb = o.transpose(0, 2, 1, 3).reshape(bsz, S, ATTN_WIDTH) @ w_pb
    m = jax.nn.sigmoid(ga) * ya + jax.nn.sigmoid(gb) * yb
    return m @ w_o


def conv_ffn(h, w_up, ffn_conv_w, ffn_conv_b, w_down):
    u = causal_dwconv(h @ w_up, ffn_conv_w, ffn_conv_b)
    uv, ug = jnp.split(u, 2, axis=-1)
    return (jax.nn.silu(ug) * uv) @ w_down


def setup_inputs(seed: int = 0) -> dict:
    key = jax.random.key(seed)
    ks = jax.random.split(key, 24)
    f32 = jnp.float32

    def nrm(k, shape, scale):
        return jax.random.normal(k, shape, f32) * scale

    L, D, C, A, F = DEPTH, D_MODEL, CONV_CH, ATTN_WIDTH, D_FF
    return {
        "x": nrm(ks[0], (BATCH, SEQ, D), 1.0),
        "c": nrm(ks[1], (BATCH, D), 1.0),
        "ada_w": nrm(ks[2], (L, D, 6 * D), 0.5 * D ** -0.5),
        "ada_b": nrm(ks[3], (L, 6 * D), 0.02),
        "norm1_g": 1.0 + nrm(ks[4], (L, D), 0.05),
        "w_in": nrm(ks[5], (L, D, IN_WIDTH), D ** -0.5),
        "conv_w": nrm(ks[6], (L, CONV_KERNEL, C), CONV_KERNEL ** -0.5),
        "conv_b": nrm(ks[7], (L, C), 0.02),
        "conv_ln_g": 1.0 + nrm(ks[8], (L, C), 0.05),
        "conv_ln_b": nrm(ks[9], (L, C), 0.02),
        "q_norm_g": 1.0 + nrm(ks[10], (L, HEAD_DIM), 0.05),
        "k_norm_g": 1.0 + nrm(ks[11], (L, HEAD_DIM), 0.05),
        "w_pa": nrm(ks[12], (L, C, D), C ** -0.5),
        "w_pb": nrm(ks[13], (L, A, D), A ** -0.5),
        "w_o": nrm(ks[14], (L, D, D), D ** -0.5),
        "norm2_g": 1.0 + nrm(ks[15], (L, D), 0.05),
        "w_up": nrm(ks[16], (L, D, 2 * F), D ** -0.5),
        "ffn_conv_w": nrm(ks[17], (L, FFN_CONV, 2 * F), FFN_CONV ** -0.5),
        "ffn_conv_b": nrm(ks[18], (L, 2 * F), 0.02),
        "w_down": nrm(ks[19], (L, F, D), F ** -0.5),
        "rel_bias": nrm(ks[20], (N_HEADS, NUM_BUCKETS), 0.5),
    }


def reference(x, c, ada_w, ada_b, norm1_g, w_in, conv_w, conv_b, conv_ln_g, conv_ln_b,
              q_norm_g, k_norm_g, w_pa, w_pb, w_o, norm2_g, w_up, ffn_conv_w, ffn_conv_b,
              w_down, rel_bias):
    cond = jax.nn.silu(c)
    for l in range(DEPTH):
        mod = (cond @ ada_w[l] + ada_b[l])[:, None, :]
        sh1, sc1, g1, sh2, sc2, g2 = jnp.split(mod, 6, axis=-1)
        h = rmsnorm(x, norm1_g[l]) * (1.0 + sc1) + sh1
        x = x + g1 * mixer(h, w_in[l], conv_w[l], conv_b[l], conv_ln_g[l], conv_ln_b[l],
                           q_norm_g[l], k_norm_g[l], w_pa[l], w_pb[l], w_o[l], rel_bias)
        h = rmsnorm(x, norm2_g[l]) * (1.0 + sc2) + sh2
        x = x + g2 * conv_ffn(h, w_up[l], ffn_conv_w[l], ffn_conv_b[l], w_down[l])
    return x
```

```python
import functools
import math

import jax
import jax.numpy as jnp
from jax import lax
from jax.experimental import pallas as pl
from jax.experimental.pallas import tpu as pltpu

F32 = jnp.float32
BF16 = jnp.bfloat16

D_MODEL = 1024
DEPTH = 4
CONV_CH = 512
CONV_KERNEL = 31
HEAD_DIM = 64
N_HEADS = 8
ATTN_WIDTH = 512
MOBA_BLOCK = 256
MOBA_TOPK = 3
NUM_BUCKETS = 32
MAX_DISTANCE = 2048
D_FF = 2816
FFN_CONV = 3
EPS = 1e-6
IN_WIDTH = 2 * CONV_CH + 3 * ATTN_WIDTH + 2 * D_MODEL

LANES = 128
SUBLANES = 8
HEAD_PAIRS = N_HEADS // 2
LOG2E = math.log2(math.e)
QK_SCALE = HEAD_DIM ** -0.5
NEG = -30000.0
FAR_BLOCKS = -(-(MAX_DISTANCE + MOBA_BLOCK) // MOBA_BLOCK)
N_BIAS_TILES = FAR_BLOCKS + 1
CONV_HALO = 32
FFN_HALO = SUBLANES
VMEM_LIMIT = 56 * 1024 * 1024

TM = 512
FF_CHUNK = 256


def _sigmoid(x):
    return 1.0 / (1.0 + jnp.exp(-x))


def _params(*sem):
    return pltpu.CompilerParams(dimension_semantics=sem, vmem_limit_bytes=VMEM_LIMIT)


def _const_spec(shape):
    nd = len(shape)
    return pl.BlockSpec(shape, lambda *_: (0,) * nd, pipeline_mode=pl.Buffered(1))


def _mod_kernel(c_ref, w_ref, b_ref, o_ref):
    c = c_ref[...]
    cond = c * _sigmoid(c)
    o_ref[0] = jnp.dot(cond, w_ref[0], precision=lax.Precision.HIGHEST,
                       preferred_element_type=F32) + b_ref[0]


def _modulation(c, ada_w, ada_b):
    bsz = c.shape[0]
    c8 = jnp.pad(c, ((0, SUBLANES - bsz), (0, 0)))
    tn = 1536
    n = 6 * D_MODEL
    out = pl.pallas_call(
        _mod_kernel,
        grid=(DEPTH, n // tn),
        in_specs=[
            pl.BlockSpec((SUBLANES, D_MODEL), lambda l, j: (0, 0)),
            pl.BlockSpec((1, D_MODEL, tn), lambda l, j: (l, 0, j)),
            pl.BlockSpec((1, 1, tn), lambda l, j: (l, 0, j)),
        ],
        out_specs=pl.BlockSpec((1, SUBLANES, tn), lambda l, j: (l, 0, j)),
        out_shape=jax.ShapeDtypeStruct((DEPTH, SUBLANES, n), F32),
        compiler_params=_params("arbitrary", "arbitrary"),
        name="adaln_modulation",
    )(c8, ada_w, ada_b.reshape(DEPTH, 1, n))
    mod = out[:, :bsz].reshape(DEPTH, bsz, 6, D_MODEL)
    return jnp.pad(mod, ((0, 0), (0, 0), (0, SUBLANES - 6), (0, 0)))


def _mod_rmsnorm(x, g, shift, scale):
    ms = jnp.mean(x * x, axis=-1, keepdims=True)
    return (x * lax.rsqrt(ms + EPS) * g) * (1.0 + scale) + shift


def _inproj_kernel(x_ref, mod_ref, g_ref, w_ref, zc_ref, zqkv_ref, zg_ref):
    mod = mod_ref[0]
    h = _mod_rmsnorm(x_ref[0], g_ref[...], mod[0:1], mod[1:2]).astype(BF16)
    tn = 512
    for n0 in range(0, IN_WIDTH, tn):
        z = jnp.dot(h, w_ref[:, n0:n0 + tn], preferred_element_type=F32)
        if n0 < 2 * CONV_CH:
            zc_ref[0, :, n0:n0 + tn] = z
        elif n0 < 2 * CONV_CH + 3 * ATTN_WIDTH:
            o = n0 - 2 * CONV_CH
            zqkv_ref[0, :, o:o + tn] = z
        else:
            o = n0 - 2 * CONV_CH - 3 * ATTN_WIDTH
            zg_ref[0, :, o:o + tn] = z


def _inproj(x, mod_l, g, w):
    bsz, seq, _ = x.shape
    widths = (2 * CONV_CH, 3 * ATTN_WIDTH, 2 * D_MODEL)
    return pl.pallas_call(
        _inproj_kernel,
        grid=(bsz, seq // TM),
        in_specs=[
            pl.BlockSpec((1, TM, D_MODEL), lambda b, t: (b, t, 0)),
            pl.BlockSpec((1, SUBLANES, D_MODEL), lambda b, t: (b, 0, 0)),
            _const_spec((1, D_MODEL)),
            _const_spec((D_MODEL, IN_WIDTH)),
        ],
        out_specs=[pl.BlockSpec((1, TM, wd), lambda b, t: (b, t, 0)) for wd in widths],
        out_shape=[jax.ShapeDtypeStruct((bsz, seq, wd), F32) for wd in widths],
        compiler_params=_params("parallel", "arbitrary"),
        name="norm_inproj",
    )(x, mod_l, g, w)


CONV_ROWS = 32


def _convmod_kernel(zc_ref, halo_ref, ga_ref, cw_ref, cb_ref, lg_ref, lb_ref, wpa_ref, o_ref,
                    a_ext, y_buf):
    t = pl.program_id(1)
    zc = zc_ref[0]
    a_ext[CONV_HALO:, :] = zc[:, :CONV_CH] * _sigmoid(zc[:, CONV_CH:])
    zh = halo_ref[0]
    ah = zh[:, :CONV_CH] * _sigmoid(zh[:, CONV_CH:])
    a_ext[:CONV_HALO, :] = jnp.where(t > 0, ah, 0.0)
    cw = cw_ref[...]
    cb = cb_ref[...]
    first = CONV_HALO - (CONV_KERNEL - 1)
    for r0 in range(0, TM, CONV_ROWS):
        acc = jnp.broadcast_to(cb, (CONV_ROWS, CONV_CH))
        for j in range(CONV_KERNEL):
            acc = acc + cw[j:j + 1] * a_ext[r0 + first + j:r0 + first + j + CONV_ROWS, :]
        y_buf[r0:r0 + CONV_ROWS, :] = acc
    y = y_buf[...]
    mu = jnp.mean(y, axis=-1, keepdims=True)
    yc = y - mu
    var = jnp.mean(yc * yc, axis=-1, keepdims=True)
    yn = yc * lax.rsqrt(var + EPS) * lg_ref[...] + lb_ref[...]
    act = (yn * _sigmoid(yn)).astype(BF16)
    ya = jnp.dot(act, wpa_ref[...], preferred_element_type=F32)
    o_ref[0] = _sigmoid(ga_ref[0]) * ya


def _convmod(zc, zg, cw, cb, lg, lb, wpa):
    bsz, seq, _ = zc.shape
    per = TM // CONV_HALO
    return pl.pallas_call(
        _convmod_kernel,
        grid=(bsz, seq // TM),
        in_specs=[
            pl.BlockSpec((1, TM, 2 * CONV_CH), lambda b, t: (b, t, 0)),
            pl.BlockSpec((1, CONV_HALO, 2 * CONV_CH), lambda b, t: (b, jnp.maximum(t * per - 1, 0), 0)),
            pl.BlockSpec((1, TM, D_MODEL), lambda b, t: (b, t, 0)),
            _const_spec((CONV_HALO, CONV_CH)),
            _const_spec((1, CONV_CH)),
            _const_spec((1, CONV_CH)),
            _const_spec((1, CONV_CH)),
            _const_spec((CONV_CH, D_MODEL)),
        ],
        out_specs=pl.BlockSpec((1, TM, D_MODEL), lambda b, t: (b, t, 0)),
        out_shape=jax.ShapeDtypeStruct((bsz, seq, D_MODEL), F32),
        scratch_shapes=[pltpu.VMEM((TM + CONV_HALO, CONV_CH), F32), pltpu.VMEM((TM, CONV_CH), F32)],
        compiler_params=_params("parallel", "arbitrary"),
        name="conv_module",
    )(zc, zc, zg, cw, cb, lg, lb, wpa)


def _attn_prep_kernel(z_ref, gq_ref, gk_ref, qn_ref, ka_ref, v_ref, km_ref):
    blk = pl.program_id(1)
    lane = lax.broadcasted_iota(jnp.int32, (MOBA_BLOCK, LANES), 1)
    low = lane < HEAD_DIM

    def head_norm(tv, g):
        t2 = tv * tv
        s0 = jnp.sum(jnp.where(low, t2, 0.0), axis=-1, keepdims=True)
        s1 = jnp.sum(jnp.where(low, 0.0, t2), axis=-1, keepdims=True)
        ms = jnp.where(low, s0, s1) * (1.0 / HEAD_DIM)
        return tv * lax.rsqrt(ms + EPS) * g

    onehot = jnp.where(lane == blk, 1.0, 0.0).astype(BF16)
    for p in range(HEAD_PAIRS):
        c0 = p * LANES
        qn_ref[0, :, c0:c0 + LANES] = head_norm(z_ref[0, :, c0:c0 + LANES], gq_ref[...])
        kn = head_norm(z_ref[0, :, ATTN_WIDTH + c0:ATTN_WIDTH + c0 + LANES], gk_ref[...])
        km_ref[0, 0, p:p + 1, :] = jnp.mean(kn, axis=0, keepdims=True)
        ka_ref[0, p, :, :LANES] = kn.astype(BF16)
        ka_ref[0, p, :, LANES:] = onehot
        v_ref[0, :, c0:c0 + LANES] = z_ref[0, :, 2 * ATTN_WIDTH + c0:2 * ATTN_WIDTH + c0 + LANES].astype(BF16)


def _attn_prep(zqkv, gq2, gk2):
    bsz, seq, _ = zqkv.shape
    nb = seq // MOBA_BLOCK
    return pl.pallas_call(
        _attn_prep_kernel,
        grid=(bsz, nb),
        in_specs=[
            pl.BlockSpec((1, MOBA_BLOCK, 3 * ATTN_WIDTH), lambda b, i: (b, i, 0)),
            _const_spec((1, LANES)),
            _const_spec((1, LANES)),
        ],
        out_specs=[
            pl.BlockSpec((1, MOBA_BLOCK, ATTN_WIDTH), lambda b, i: (b, i, 0)),
            pl.BlockSpec((1, HEAD_PAIRS, MOBA_BLOCK, 2 * LANES), lambda b, i: (b, 0, i, 0)),
            pl.BlockSpec((1, MOBA_BLOCK, ATTN_WIDTH), lambda b, i: (b, i, 0)),
            pl.BlockSpec((1, 1, HEAD_PAIRS, LANES), lambda b, i: (b, i, 0, 0)),
        ],
        out_shape=[
            jax.ShapeDtypeStruct((bsz, seq, ATTN_WIDTH), F32),
            jax.ShapeDtypeStruct((bsz, HEAD_PAIRS, seq, 2 * LANES), BF16),
            jax.ShapeDtypeStruct((bsz, seq, ATTN_WIDTH), BF16),
            jax.ShapeDtypeStruct((bsz, nb, HEAD_PAIRS, LANES), F32),
        ],
        compiler_params=_params("parallel", "arbitrary"),
        name="attn_prep",
    )(zqkv, gq2, gk2)


def _bias_tile_kernel(rb_ref, o_ref):
    h = pl.program_id(0)
    dblk = pl.program_id(1)
    r = lax.broadcasted_iota(jnp.int32, (MOBA_BLOCK, MOBA_BLOCK), 0)
    c = lax.broadcasted_iota(jnp.int32, (MOBA_BLOCK, MOBA_BLOCK), 1)
    d = dblk * MOBA_BLOCK + r - c
    dist = jnp.maximum(d, 0)
    max_exact = NUM_BUCKETS // 2
    nf = jnp.maximum(dist, 1).astype(F32)
    large = max_exact + (jnp.log(nf / max_exact) / math.log(MAX_DISTANCE / max_exact)
                         * (NUM_BUCKETS - max_exact)).astype(jnp.int32)
    large = jnp.minimum(large, NUM_BUCKETS - 1)
    bucket = jnp.where(dist < max_exact, dist, large)
    val = jnp.zeros((MOBA_BLOCK, MOBA_BLOCK), F32)
    for b in range(NUM_BUCKETS):
        val = jnp.where(bucket == b, rb_ref[h, b], val)
    o_ref[0, 0] = jnp.where(d >= 0, val * LOG2E, NEG)


def _bias_tiles(rel_bias):
    return pl.pallas_call(
        _bias_tile_kernel,
        grid=(N_HEADS, N_BIAS_TILES),
        in_specs=[pl.BlockSpec(memory_space=pltpu.SMEM)],
        out_specs=pl.BlockSpec((1, 1, MOBA_BLOCK, MOBA_BLOCK), lambda h, d: (h, d, 0, 0)),
        out_shape=jax.ShapeDtypeStruct((N_HEADS, N_BIAS_TILES, MOBA_BLOCK, MOBA_BLOCK), F32),
        compiler_params=_params("arbitrary", "arbitrary"),
        name="t5_bias_tiles",
    )(rel_bias)


def _attn_kernel(qn_ref, ka_ref, v_ref, km_ref, bias_ref, o_ref):
    i = pl.program_id(2)
    lane = lax.broadcasted_iota(jnp.int32, (MOBA_BLOCK, LANES), 1)
    qn = qn_ref[0]
    past = lane < i
    accs = []
    for hh in range(2):
        hmask = (lane >= hh * HEAD_DIM) & (lane < (hh + 1) * HEAD_DIM)
        qh = jnp.where(hmask, qn, 0.0)
        gate = lax.dot_general(qh, km_ref[0, 0], (((1,), (1,)), ((), ())),
                               precision=lax.Precision.HIGHEST, preferred_element_type=F32)
        g = jnp.where(past, gate, -jnp.inf)
        sel = lane == i
        for _ in range(MOBA_TOPK):
            mx = jnp.max(g, axis=-1, keepdims=True)
            idx = jnp.min(jnp.where(g == mx, lane, LANES), axis=-1, keepdims=True)
            hit = lane == idx
            sel = sel | (hit & past)
            g = jnp.where(hit, -jnp.inf, g)
        msk = jnp.where(sel, 0.0, NEG)
        q_aug = jnp.concatenate([(qh * (QK_SCALE * LOG2E)).astype(BF16), msk.astype(BF16)], axis=-1)

        def tile(j, carry, q_aug=q_aug, hh=hh):
            m, l, acc = carry
            start = pl.multiple_of(j * MOBA_BLOCK, MOBA_BLOCK)
            kj = ka_ref[0, 0, pl.ds(start, MOBA_BLOCK), :]
            s = lax.dot_general(q_aug, kj, (((1,), (1,)), ((), ())), preferred_element_type=F32)
            s = s + bias_ref[hh, jnp.minimum(i - j, FAR_BLOCKS)]
            m_new = jnp.maximum(m, jnp.max(s, axis=-1, keepdims=True))
            alpha = jnp.exp2(m - m_new)
            p = jnp.exp2(s - m_new)
            l = alpha * l + jnp.sum(p, axis=-1, keepdims=True)
            vj = v_ref[0, pl.ds(start, MOBA_BLOCK), :]
            acc = alpha * acc + jnp.dot(p.astype(BF16), vj, preferred_element_type=F32)
            return m_new, l, acc

        init = (jnp.full((MOBA_BLOCK, 1), -jnp.inf, F32), jnp.zeros((MOBA_BLOCK, 1), F32),
                jnp.zeros((MOBA_BLOCK, LANES), F32))
        _, l, acc = lax.fori_loop(0, i + 1, tile, init)
        accs.append(acc * (1.0 / l))
    o_ref[0] = jnp.where(lane < HEAD_DIM, accs[0], accs[1]).astype(BF16)


def _attention(qn, ka, vb, km, bias):
    bsz, seq, _ = qn.shape
    nb = seq // MOBA_BLOCK
    return pl.pallas_call(
        _attn_kernel,
        grid=(bsz, HEAD_PAIRS, nb),
        in_specs=[
            pl.BlockSpec((1, MOBA_BLOCK, LANES), lambda b, p, i: (b, i, p)),
            pl.BlockSpec((1, 1, seq, 2 * LANES), lambda b, p, i: (b, p, 0, 0)),
            pl.BlockSpec((1, seq, LANES), lambda b, p, i: (b, 0, p)),
            pl.BlockSpec((1, 1, LANES, LANES), lambda b, p, i: (b, p, 0, 0)),
            pl.BlockSpec((2, N_BIAS_TILES, MOBA_BLOCK, MOBA_BLOCK), lambda b, p, i: (p, 0, 0, 0)),
        ],
        out_specs=pl.BlockSpec((1, MOBA_BLOCK, LANES), lambda b, p, i: (b, i, p)),
        out_shape=jax.ShapeDtypeStruct((bsz, seq, ATTN_WIDTH), BF16),
        compiler_params=_params("parallel", "parallel", "arbitrary"),
        name="moba_attention",
    )(qn, ka, vb, km, bias)


def _merge_kernel(o_ref, ma_ref, gb_ref, x_ref, mod_ref, wpb_ref, wo_ref, out_ref):
    yb = jnp.dot(o_ref[0], wpb_ref[...], preferred_element_type=F32)
    m = (ma_ref[0] + _sigmoid(gb_ref[0]) * yb).astype(BF16)
    y = jnp.dot(m, wo_ref[...], preferred_element_type=F32)
    out_ref[0] = x_ref[0] + mod_ref[0][2:3] * y


def _merge(o, ma, zg, x, mod_l, wpb, wo):
    bsz, seq, _ = x.shape
    tok = lambda b, t: (b, t, 0)
    return pl.pallas_call(
        _merge_kernel,
        grid=(bsz, seq // TM),
        in_specs=[
            pl.BlockSpec((1, TM, ATTN_WIDTH), tok),
            pl.BlockSpec((1, TM, D_MODEL), tok),
            pl.BlockSpec((1, TM, D_MODEL), lambda b, t: (b, t, 1)),
            pl.BlockSpec((1, TM, D_MODEL), tok),
            pl.BlockSpec((1, SUBLANES, D_MODEL), lambda b, t: (b, 0, 0)),
            _const_spec((ATTN_WIDTH, D_MODEL)),
            _const_spec((D_MODEL, D_MODEL)),
        ],
        out_specs=pl.BlockSpec((1, TM, D_MODEL), tok),
        out_shape=jax.ShapeDtypeStruct((bsz, seq, D_MODEL), F32),
        compiler_params=_params("parallel", "arbitrary"),
        name="merge_outproj",
    )(o, ma, zg, x, mod_l, wpb, wo)


def _ffn_kernel(x_ref, halo_ref, mod_ref, g_ref, wup_ref, fcw_ref, fcb_ref, wdn_ref, out_ref):
    t = pl.program_id(1)
    mod = mod_ref[0]
    x = x_ref[0]
    xe = jnp.concatenate([halo_ref[0], x], axis=0)
    h = _mod_rmsnorm(xe, g_ref[...], mod[3:4], mod[4:5])
    row = lax.broadcasted_iota(jnp.int32, (FFN_HALO + TM, 1), 0)
    h = jnp.where((row >= FFN_HALO) | (t > 0), h, 0.0).astype(BF16)

    def conv(u, col):
        w = fcw_ref[:, col:col + FF_CHUNK]
        y = w[2:3] * u + w[1:2] * pltpu.roll(u, 1, 0) + w[0:1] * pltpu.roll(u, 2, 0)
        return y[FFN_HALO:] + fcb_ref[:, col:col + FF_CHUNK]

    acc = jnp.zeros((TM, D_MODEL), F32)
    for c0 in range(0, D_FF, FF_CHUNK):
        uv = conv(jnp.dot(h, wup_ref[:, c0:c0 + FF_CHUNK], preferred_element_type=F32), c0)
        ug = conv(jnp.dot(h, wup_ref[:, D_FF + c0:D_FF + c0 + FF_CHUNK], preferred_element_type=F32),
                  D_FF + c0)
        a = (ug * _sigmoid(ug) * uv).astype(BF16)
        acc = acc + jnp.dot(a, wdn_ref[c0:c0 + FF_CHUNK, :], preferred_element_type=F32)
    out_ref[0] = x + mod[5:6] * acc


def _ffn(x, mod_l, g, wup, fcw, fcb, wdn):
    bsz, seq, _ = x.shape
    per = TM // FFN_HALO
    return pl.pallas_call(
        _ffn_kernel,
        grid=(bsz, seq // TM),
        in_specs=[
            pl.BlockSpec((1, TM, D_MODEL), lambda b, t: (b, t, 0)),
            pl.BlockSpec((1, FFN_HALO, D_MODEL), lambda b, t: (b, jnp.maximum(t * per - 1, 0), 0)),
            pl.BlockSpec((1, SUBLANES, D_MODEL), lambda b, t: (b, 0, 0)),
            _const_spec((1, D_MODEL)),
            _const_spec((D_MODEL, 2 * D_FF)),
            _const_spec((SUBLANES, 2 * D_FF)),
            _const_spec((1, 2 * D_FF)),
            _const_spec((D_FF, D_MODEL)),
        ],
        out_specs=pl.BlockSpec((1, TM, D_MODEL), lambda b, t: (b, t, 0)),
        out_shape=jax.ShapeDtypeStruct((bsz, seq, D_MODEL), F32),
        compiler_params=_params("parallel", "arbitrary"),
        name="conv_ffn",
    )(x, x, mod_l, g, wup, fcw, fcb, wdn)


def kernel(x, c, ada_w, ada_b, norm1_g, w_in, conv_w, conv_b, conv_ln_g, conv_ln_b, q_norm_g, k_norm_g,
           w_pa, w_pb, w_o, norm2_g, w_up, ffn_conv_w, ffn_conv_b, w_down, rel_bias):
    bsz, seq, _ = x.shape
    nb = seq // MOBA_BLOCK
    mod = _modulation(c, ada_w, ada_b)
    bias = _bias_tiles(rel_bias)
    cw = jnp.pad(conv_w, ((0, 0), (0, CONV_HALO - CONV_KERNEL), (0, 0)))
    fcw = jnp.pad(ffn_conv_w, ((0, 0), (0, SUBLANES - FFN_CONV), (0, 0)))
    for l in range(DEPTH):
        zc, zqkv, zg = _inproj(x, mod[l], norm1_g[l][None], w_in[l].astype(BF16))
        ma = _convmod(zc, zg, cw[l], conv_b[l][None], conv_ln_g[l][None], conv_ln_b[l][None],
                      w_pa[l].astype(BF16))
        qn, ka, vb, km = _attn_prep(zqkv, jnp.tile(q_norm_g[l], 2)[None], jnp.tile(k_norm_g[l], 2)[None])
        km = jnp.pad(km.transpose(0, 2, 1, 3), ((0, 0), (0, 0), (0, LANES - nb), (0, 0)))
        o = _attention(qn, ka, vb, km, bias)
        x = _merge(o, ma, zg, x, mod[l], w_pb[l].astype(BF16), w_o[l].astype(BF16))
        x = _ffn(x, mod[l], norm2_g[l][None], w_up[l].astype(BF16), fcw[l], ffn_conv_b[l][None],
                 w_down[l].astype(BF16))
    return x
```

```python
import functools
import math

import jax
import jax.numpy as jnp
from jax import lax
from jax.experimental import pallas as pl
from jax.experimental.pallas import tpu as pltpu

F32 = jnp.float32
BF16 = jnp.bfloat16

D_MODEL = 1024
DEPTH = 4
CONV_CH = 512
CONV_KERNEL = 31
HEAD_DIM = 64
N_HEADS = 8
ATTN_WIDTH = 512
MOBA_BLOCK = 256
MOBA_TOPK = 3
NUM_BUCKETS = 32
MAX_DISTANCE = 2048
D_FF = 2816
FFN_CONV = 3
EPS = 1e-6
IN_WIDTH = 2 * CONV_CH + 3 * ATTN_WIDTH + 2 * D_MODEL

LANES = 128
SUBLANES = 8
HEAD_PAIRS = N_HEADS // 2
LOG2E = math.log2(math.e)
QK_SCALE = HEAD_DIM ** -0.5
NEG = -30000.0
FAR_BLOCKS = -(-(MAX_DISTANCE + MOBA_BLOCK) // MOBA_BLOCK)
N_BIAS_TILES = FAR_BLOCKS + 1
CONV_HALO = 32
FFN_HALO = SUBLANES
VMEM_LIMIT = 56 * 1024 * 1024

TM = 512
FF_CHUNK = 256
KB_GROUP = 4


def _sigmoid(x):
    return 1.0 / (1.0 + jnp.exp(-x))


def _params(*sem):
    return pltpu.CompilerParams(dimension_semantics=sem, vmem_limit_bytes=VMEM_LIMIT)


def _const_spec(shape):
    nd = len(shape)
    return pl.BlockSpec(shape, lambda *_: (0,) * nd, pipeline_mode=pl.Buffered(1))


def _mod_kernel(c_ref, w_ref, b_ref, o_ref):
    c = c_ref[...]
    cond = c * _sigmoid(c)
    o_ref[0] = jnp.dot(cond, w_ref[0], precision=lax.Precision.HIGHEST,
                       preferred_element_type=F32) + b_ref[0]


def _modulation(c, ada_w, ada_b):
    bsz = c.shape[0]
    c8 = jnp.pad(c, ((0, SUBLANES - bsz), (0, 0)))
    tn = 1536
    n = 6 * D_MODEL
    out = pl.pallas_call(
        _mod_kernel,
        grid=(DEPTH, n // tn),
        in_specs=[
            pl.BlockSpec((SUBLANES, D_MODEL), lambda l, j: (0, 0)),
            pl.BlockSpec((1, D_MODEL, tn), lambda l, j: (l, 0, j)),
            pl.BlockSpec((1, 1, tn), lambda l, j: (l, 0, j)),
        ],
        out_specs=pl.BlockSpec((1, SUBLANES, tn), lambda l, j: (l, 0, j)),
        out_shape=jax.ShapeDtypeStruct((DEPTH, SUBLANES, n), F32),
        compiler_params=_params("arbitrary", "arbitrary"),
        name="adaln_modulation",
    )(c8, ada_w, ada_b.reshape(DEPTH, 1, n))
    mod = out[:, :bsz].reshape(DEPTH, bsz, 6, D_MODEL)
    return jnp.pad(mod, ((0, 0), (0, 0), (0, SUBLANES - 6), (0, 0)))


def _mod_rmsnorm(x, g, shift, scale):
    ms = jnp.mean(x * x, axis=-1, keepdims=True)
    return (x * lax.rsqrt(ms + EPS) * g) * (1.0 + scale) + shift


def _inproj_kernel(x_ref, mod_ref, g_ref, w_ref, zc_ref, zqkv_ref, zg_ref):
    mod = mod_ref[0]
    h = _mod_rmsnorm(x_ref[0], g_ref[...], mod[0:1], mod[1:2]).astype(BF16)
    tn = 512
    for n0 in range(0, IN_WIDTH, tn):
        z = jnp.dot(h, w_ref[:, n0:n0 + tn], preferred_element_type=F32)
        if n0 < 2 * CONV_CH:
            zc_ref[0, :, n0:n0 + tn] = z
        elif n0 < 2 * CONV_CH + 3 * ATTN_WIDTH:
            o = n0 - 2 * CONV_CH
            zqkv_ref[0, :, o:o + tn] = z
        else:
            o = n0 - 2 * CONV_CH - 3 * ATTN_WIDTH
            zg_ref[0, :, o:o + tn] = z


def _inproj(x, mod_l, g, w):
    bsz, seq, _ = x.shape
    widths = (2 * CONV_CH, 3 * ATTN_WIDTH, 2 * D_MODEL)
    return pl.pallas_call(
        _inproj_kernel,
        grid=(bsz, seq // TM),
        in_specs=[
            pl.BlockSpec((1, TM, D_MODEL), lambda b, t: (b, t, 0)),
            pl.BlockSpec((1, SUBLANES, D_MODEL), lambda b, t: (b, 0, 0)),
            _const_spec((1, D_MODEL)),
            _const_spec((D_MODEL, IN_WIDTH)),
        ],
        out_specs=[pl.BlockSpec((1, TM, wd), lambda b, t: (b, t, 0)) for wd in widths],
        out_shape=[jax.ShapeDtypeStruct((bsz, seq, wd), F32) for wd in widths],
        compiler_params=_params("parallel", "arbitrary"),
        name="norm_inproj",
    )(x, mod_l, g, w)


CONV_ROWS = 32


def _convmod_kernel(zc_ref, halo_ref, ga_ref, cw_ref, cb_ref, lg_ref, lb_ref, wpa_ref, o_ref,
                    a_ext, y_buf):
    t = pl.program_id(1)
    zc = zc_ref[0]
    a_ext[CONV_HALO:, :] = zc[:, :CONV_CH] * _sigmoid(zc[:, CONV_CH:])
    zh = halo_ref[0]
    ah = zh[:, :CONV_CH] * _sigmoid(zh[:, CONV_CH:])
    a_ext[:CONV_HALO, :] = jnp.where(t > 0, ah, 0.0)
    cw = cw_ref[...]
    cb = cb_ref[...]
    first = CONV_HALO - (CONV_KERNEL - 1)
    for r0 in range(0, TM, CONV_ROWS):
        acc = jnp.broadcast_to(cb, (CONV_ROWS, CONV_CH))
        for j in range(CONV_KERNEL):
            acc = acc + cw[j:j + 1] * a_ext[r0 + first + j:r0 + first + j + CONV_ROWS, :]
        y_buf[r0:r0 + CONV_ROWS, :] = acc
    y = y_buf[...]
    mu = jnp.mean(y, axis=-1, keepdims=True)
    yc = y - mu
    var = jnp.mean(yc * yc, axis=-1, keepdims=True)
    yn = yc * lax.rsqrt(var + EPS) * lg_ref[...] + lb_ref[...]
    act = (yn * _sigmoid(yn)).astype(BF16)
    ya = jnp.dot(act, wpa_ref[...], preferred_element_type=F32)
    o_ref[0] = _sigmoid(ga_ref[0]) * ya


def _convmod(zc, zg, cw, cb, lg, lb, wpa):
    bsz, seq, _ = zc.shape
    per = TM // CONV_HALO
    return pl.pallas_call(
        _convmod_kernel,
        grid=(bsz, seq // TM),
        in_specs=[
            pl.BlockSpec((1, TM, 2 * CONV_CH), lambda b, t: (b, t, 0)),
            pl.BlockSpec((1, CONV_HALO, 2 * CONV_CH), lambda b, t: (b, jnp.maximum(t * per - 1, 0), 0)),
            pl.BlockSpec((1, TM, D_MODEL), lambda b, t: (b, t, 0)),
            _const_spec((CONV_HALO, CONV_CH)),
            _const_spec((1, CONV_CH)),
            _const_spec((1, CONV_CH)),
            _const_spec((1, CONV_CH)),
            _const_spec((CONV_CH, D_MODEL)),
        ],
        out_specs=pl.BlockSpec((1, TM, D_MODEL), lambda b, t: (b, t, 0)),
        out_shape=jax.ShapeDtypeStruct((bsz, seq, D_MODEL), F32),
        scratch_shapes=[pltpu.VMEM((TM + CONV_HALO, CONV_CH), F32), pltpu.VMEM((TM, CONV_CH), F32)],
        compiler_params=_params("parallel", "arbitrary"),
        name="conv_module",
    )(zc, zc, zg, cw, cb, lg, lb, wpa)


def _attn_prep_kernel(z_ref, gq_ref, gk_ref, qt_ref, ka_ref, vt_ref, km_ref):
    blk = pl.program_id(1)
    lane = lax.broadcasted_iota(jnp.int32, (MOBA_BLOCK, LANES), 1)
    low = lane < HEAD_DIM

    def head_norm(tv, g):
        t2 = tv * tv
        s0 = jnp.sum(jnp.where(low, t2, 0.0), axis=-1, keepdims=True)
        s1 = jnp.sum(jnp.where(low, 0.0, t2), axis=-1, keepdims=True)
        ms = jnp.where(low, s0, s1) * (1.0 / HEAD_DIM)
        return tv * lax.rsqrt(ms + EPS) * g

    onehot = jnp.where(lane == blk, 1.0, 0.0).astype(BF16)
    for p in range(HEAD_PAIRS):
        c0 = p * LANES
        qt_ref[0, p, 0] = head_norm(z_ref[0, :, c0:c0 + LANES], gq_ref[...]).T
        kn = head_norm(z_ref[0, :, ATTN_WIDTH + c0:ATTN_WIDTH + c0 + LANES], gk_ref[...])
        km_ref[0, 0, p:p + 1, :] = jnp.mean(kn, axis=0, keepdims=True)
        ka_ref[0, p, :, :LANES] = kn.astype(BF16)
        ka_ref[0, p, :, LANES:] = onehot
        vt_ref[0, p, 0] = z_ref[0, :, 2 * ATTN_WIDTH + c0:2 * ATTN_WIDTH + c0 + LANES].T.astype(BF16)


def _attn_prep(zqkv, gq2, gk2):
    bsz, seq, _ = zqkv.shape
    nb = seq // MOBA_BLOCK
    tiles = (bsz, HEAD_PAIRS, nb, LANES, MOBA_BLOCK)
    tile_spec = pl.BlockSpec((1, HEAD_PAIRS, 1, LANES, MOBA_BLOCK), lambda b, i: (b, 0, i, 0, 0))
    return pl.pallas_call(
        _attn_prep_kernel,
        grid=(bsz, nb),
        in_specs=[
            pl.BlockSpec((1, MOBA_BLOCK, 3 * ATTN_WIDTH), lambda b, i: (b, i, 0)),
            _const_spec((1, LANES)),
            _const_spec((1, LANES)),
        ],
        out_specs=[
            tile_spec,
            pl.BlockSpec((1, HEAD_PAIRS, MOBA_BLOCK, 2 * LANES), lambda b, i: (b, 0, i, 0)),
            tile_spec,
            pl.BlockSpec((1, 1, HEAD_PAIRS, LANES), lambda b, i: (b, i, 0, 0)),
        ],
        out_shape=[
            jax.ShapeDtypeStruct(tiles, F32),
            jax.ShapeDtypeStruct((bsz, HEAD_PAIRS, seq, 2 * LANES), BF16),
            jax.ShapeDtypeStruct(tiles, BF16),
            jax.ShapeDtypeStruct((bsz, nb, HEAD_PAIRS, LANES), F32),
        ],
        compiler_params=_params("parallel", "arbitrary"),
        name="attn_prep",
    )(zqkv, gq2, gk2)


def _bias_tile_kernel(rb_ref, o_ref):
    h = pl.program_id(0)
    dblk = pl.program_id(1)
    k = lax.broadcasted_iota(jnp.int32, (MOBA_BLOCK, MOBA_BLOCK), 0)
    q = lax.broadcasted_iota(jnp.int32, (MOBA_BLOCK, MOBA_BLOCK), 1)
    d = dblk * MOBA_BLOCK + q - k
    dist = jnp.maximum(d, 0)
    max_exact = NUM_BUCKETS // 2
    nf = jnp.maximum(dist, 1).astype(F32)
    large = max_exact + (jnp.log(nf / max_exact) / math.log(MAX_DISTANCE / max_exact)
                         * (NUM_BUCKETS - max_exact)).astype(jnp.int32)
    large = jnp.minimum(large, NUM_BUCKETS - 1)
    bucket = jnp.where(dist < max_exact, dist, large)
    val = jnp.zeros((MOBA_BLOCK, MOBA_BLOCK), F32)
    for b in range(NUM_BUCKETS):
        val = jnp.where(bucket == b, rb_ref[h, b], val)
    o_ref[0, 0] = jnp.where(d >= 0, val * LOG2E, NEG)


def _bias_tiles(rel_bias):
    return pl.pallas_call(
        _bias_tile_kernel,
        grid=(N_HEADS, N_BIAS_TILES),
        in_specs=[pl.BlockSpec(memory_space=pltpu.SMEM)],
        out_specs=pl.BlockSpec((1, 1, MOBA_BLOCK, MOBA_BLOCK), lambda h, d: (h, d, 0, 0)),
        out_shape=jax.ShapeDtypeStruct((N_HEADS, N_BIAS_TILES, MOBA_BLOCK, MOBA_BLOCK), F32),
        compiler_params=_params("arbitrary", "arbitrary"),
        name="t5_bias_tiles",
    )(rel_bias)


def _attn_kernel(qt_ref, ka_ref, vt_ref, km_ref, bias_ref, o_ref):
    i = pl.program_id(2)
    nblk = km_ref.shape[2]
    qt = qt_ref[0, 0, 0]
    feat = lax.broadcasted_iota(jnp.int32, (LANES, MOBA_BLOCK), 0)
    blk = lax.broadcasted_iota(jnp.int32, (nblk, MOBA_BLOCK), 0)
    past = blk < i
    q_augs = []
    for hh in range(2):
        hmask = (feat >= hh * HEAD_DIM) & (feat < (hh + 1) * HEAD_DIM)
        qh = jnp.where(hmask, qt, 0.0)
        gate = jnp.dot(km_ref[0, 0], qh, precision=lax.Precision.HIGHEST, preferred_element_type=F32)
        g = jnp.where(past, gate, -jnp.inf)
        sel = blk == i
        for _ in range(MOBA_TOPK):
            mx = jnp.max(g, axis=0, keepdims=True)
            idx = jnp.min(jnp.where(g == mx, blk, nblk), axis=0, keepdims=True)
            hit = blk == idx
            sel = sel | (hit & past)
            g = jnp.where(hit, -jnp.inf, g)
        msk = jnp.concatenate([jnp.where(sel, 0.0, NEG), jnp.zeros((LANES - nblk, MOBA_BLOCK), F32)], axis=0)
        q_augs.append(jnp.concatenate([qh * (QK_SCALE * LOG2E), msk], axis=0).astype(BF16))

    def tile(j, carry, near):
        blocks = [KB_GROUP * j + u for u in range(KB_GROUP)]
        start = pl.multiple_of(j * (KB_GROUP * MOBA_BLOCK), KB_GROUP * MOBA_BLOCK)
        ks = [ka_ref[0, 0, pl.ds(start + u * MOBA_BLOCK, MOBA_BLOCK), :] for u in range(KB_GROUP)]
        vj = jnp.concatenate([vt_ref[0, 0, b] for b in blocks], axis=1)
        ss = [[jnp.dot(k, q_augs[hh], preferred_element_type=F32) for k in ks] for hh in range(2)]
        ps, stats = [], []
        for hh in range(2):
            m, l, _ = carry[hh]
            if near:
                s = [sv + bias_ref[hh, jnp.clip(i - b, 0, FAR_BLOCKS)] for sv, b in zip(ss[hh], blocks)]
                shift_by = 0.0
            else:
                s = ss[hh]
                shift_by = bias_ref[hh, FAR_BLOCKS, 0:1, 0:1]
            mx = jnp.max(functools.reduce(jnp.maximum, s), axis=0, keepdims=True)
            m_new = jnp.maximum(m, mx + shift_by)
            shift = m_new - shift_by
            alpha = jnp.exp2(m - m_new)
            p = [jnp.exp2(sv - shift) for sv in s]
            psum = jnp.sum(functools.reduce(jnp.add, p), axis=0, keepdims=True)
            stats.append((m_new, alpha * l + psum, alpha))
            ps.append(jnp.concatenate([pv.astype(BF16) for pv in p], axis=0))
        out = []
        for hh in range(2):
            m_new, l, alpha = stats[hh]
            acc = alpha * carry[hh][2] + jnp.dot(vj, ps[hh], preferred_element_type=F32)
            out.append((m_new, l, acc))
        return tuple(out)

    init = tuple((jnp.full((1, MOBA_BLOCK), -jnp.inf, F32), jnp.zeros((1, MOBA_BLOCK), F32),
                  jnp.zeros((LANES, MOBA_BLOCK), F32)) for _ in range(2))
    n_far = jnp.maximum(i - (FAR_BLOCKS - 1), 0) // KB_GROUP
    carry = lax.fori_loop(0, n_far, functools.partial(tile, near=False), init)
    carry = lax.fori_loop(n_far, i // KB_GROUP + 1, functools.partial(tile, near=True), carry)
    o0 = carry[0][2] * (1.0 / carry[0][1])
    o1 = carry[1][2] * (1.0 / carry[1][1])
    o_ref[0] = jnp.where(feat < HEAD_DIM, o0, o1).T.astype(BF16)


def _attention(qt, ka, vt, km, bias):
    bsz, _, seq, _ = ka.shape
    nb = seq // MOBA_BLOCK
    return pl.pallas_call(
        _attn_kernel,
        grid=(bsz, HEAD_PAIRS, nb),
        in_specs=[
            pl.BlockSpec((1, 1, 1, LANES, MOBA_BLOCK), lambda b, p, i: (b, p, i, 0, 0)),
            pl.BlockSpec((1, 1, seq, 2 * LANES), lambda b, p, i: (b, p, 0, 0)),
            pl.BlockSpec((1, 1, nb, LANES, MOBA_BLOCK), lambda b, p, i: (b, p, 0, 0, 0)),
            pl.BlockSpec((1, 1, nb, LANES), lambda b, p, i: (b, p, 0, 0)),
            pl.BlockSpec((2, N_BIAS_TILES, MOBA_BLOCK, MOBA_BLOCK), lambda b, p, i: (p, 0, 0, 0)),
        ],
        out_specs=pl.BlockSpec((1, MOBA_BLOCK, LANES), lambda b, p, i: (b, i, p)),
        out_shape=jax.ShapeDtypeStruct((bsz, seq, ATTN_WIDTH), BF16),
        compiler_params=_params("parallel", "parallel", "arbitrary"),
        name="moba_attention",
    )(qt, ka, vt, km, bias)


def _merge_kernel(o_ref, ma_ref, gb_ref, x_ref, mod_ref, wpb_ref, wo_ref, out_ref):
    yb = jnp.dot(o_ref[0], wpb_ref[...], preferred_element_type=F32)
    m = (ma_ref[0] + _sigmoid(gb_ref[0]) * yb).astype(BF16)
    y = jnp.dot(m, wo_ref[...], preferred_element_type=F32)
    out_ref[0] = x_ref[0] + mod_ref[0][2:3] * y


def _merge(o, ma, zg, x, mod_l, wpb, wo):
    bsz, seq, _ = x.shape
    tok = lambda b, t: (b, t, 0)
    return pl.pallas_call(
        _merge_kernel,
        grid=(bsz, seq // TM),
        in_specs=[
            pl.BlockSpec((1, TM, ATTN_WIDTH), tok),
            pl.BlockSpec((1, TM, D_MODEL), tok),
            pl.BlockSpec((1, TM, D_MODEL), lambda b, t: (b, t, 1)),
            pl.BlockSpec((1, TM, D_MODEL), tok),
            pl.BlockSpec((1, SUBLANES, D_MODEL), lambda b, t: (b, 0, 0)),
            _const_spec((ATTN_WIDTH, D_MODEL)),
            _const_spec((D_MODEL, D_MODEL)),
        ],
        out_specs=pl.BlockSpec((1, TM, D_MODEL), tok),
        out_shape=jax.ShapeDtypeStruct((bsz, seq, D_MODEL), F32),
        compiler_params=_params("parallel", "arbitrary"),
        name="merge_outproj",
    )(o, ma, zg, x, mod_l, wpb, wo)


def _ffn_kernel(x_ref, halo_ref, mod_ref, g_ref, wup_ref, fcw_ref, fcb_ref, wdn_ref, out_ref):
    t = pl.program_id(1)
    mod = mod_ref[0]
    x = x_ref[0]
    xe = jnp.concatenate([halo_ref[0], x], axis=0)
    h = _mod_rmsnorm(xe, g_ref[...], mod[3:4], mod[4:5])
    row = lax.broadcasted_iota(jnp.int32, (FFN_HALO + TM, 1), 0)
    h = jnp.where((row >= FFN_HALO) | (t > 0), h, 0.0).astype(BF16)

    def conv(u, col):
        w = fcw_ref[:, col:col + FF_CHUNK]
        y = w[2:3] * u + w[1:2] * pltpu.roll(u, 1, 0) + w[0:1] * pltpu.roll(u, 2, 0)
        return y[FFN_HALO:] + fcb_ref[:, col:col + FF_CHUNK]

    acc = jnp.zeros((TM, D_MODEL), F32)
    for c0 in range(0, D_FF, FF_CHUNK):
        uv = conv(jnp.dot(h, wup_ref[:, c0:c0 + FF_CHUNK], preferred_element_type=F32), c0)
        ug = conv(jnp.dot(h, wup_ref[:, D_FF + c0:D_FF + c0 + FF_CHUNK], preferred_element_type=F32),
                  D_FF + c0)
        a = (ug * _sigmoid(ug) * uv).astype(BF16)
        acc = acc + jnp.dot(a, wdn_ref[c0:c0 + FF_CHUNK, :], preferred_element_type=F32)
    out_ref[0] = x + mod[5:6] * acc


def _ffn(x, mod_l, g, wup, fcw, fcb, wdn):
    bsz, seq, _ = x.shape
    per = TM // FFN_HALO
    return pl.pallas_call(
        _ffn_kernel,
        grid=(bsz, seq // TM),
        in_specs=[
            pl.BlockSpec((1, TM, D_MODEL), lambda b, t: (b, t, 0)),
            pl.BlockSpec((1, FFN_HALO, D_MODEL), lambda b, t: (b, jnp.maximum(t * per - 1, 0), 0)),
            pl.BlockSpec((1, SUBLANES, D_MODEL), lambda b, t: (b, 0, 0)),
            _const_spec((1, D_MODEL)),
            _const_spec((D_MODEL, 2 * D_FF)),
            _const_spec((SUBLANES, 2 * D_FF)),
            _const_spec((1, 2 * D_FF)),
            _const_spec((D_FF, D_MODEL)),
        ],
        out_specs=pl.BlockSpec((1, TM, D_MODEL), lambda b, t: (b, t, 0)),
        out_shape=jax.ShapeDtypeStruct((bsz, seq, D_MODEL), F32),
        compiler_params=_params("parallel", "arbitrary"),
        name="conv_ffn",
    )(x, x, mod_l, g, wup, fcw, fcb, wdn)


def kernel(x, c, ada_w, ada_b, norm1_g, w_in, conv_w, conv_b, conv_ln_g, conv_ln_b, q_norm_g, k_norm_g,
           w_pa, w_pb, w_o, norm2_g, w_up, ffn_conv_w, ffn_conv_b, w_down, rel_bias):
    bsz, seq, _ = x.shape
    nb = seq // MOBA_BLOCK
    mod = _modulation(c, ada_w, ada_b)
    bias = _bias_tiles(rel_bias)
    cw = jnp.pad(conv_w, ((0, 0), (0, CONV_HALO - CONV_KERNEL), (0, 0)))
    fcw = jnp.pad(ffn_conv_w, ((0, 0), (0, SUBLANES - FFN_CONV), (0, 0)))
    for l in range(DEPTH):
        zc, zqkv, zg = _inproj(x, mod[l], norm1_g[l][None], w_in[l].astype(BF16))
        ma = _convmod(zc, zg, cw[l], conv_b[l][None], conv_ln_g[l][None], conv_ln_b[l][None],
                      w_pa[l].astype(BF16))
        qt, ka, vt, km = _attn_prep(zqkv, jnp.tile(q_norm_g[l], 2)[None], jnp.tile(k_norm_g[l], 2)[None])
        o = _attention(qt, ka, vt, km.transpose(0, 2, 1, 3), bias)
        x = _merge(o, ma, zg, x, mod[l], w_pb[l].astype(BF16), w_o[l].astype(BF16))
        x = _ffn(x, mod[l], norm2_g[l][None], w_up[l].astype(BF16), fcw[l], ffn_conv_b[l][None],
                 w_down[l].astype(BF16))
    return x
```

```python
import functools
import math

import jax
import jax.numpy as jnp
from jax import lax
from jax.experimental import pallas as pl
from jax.experimental.pallas import tpu as pltpu

F32 = jnp.float32
BF16 = jnp.bfloat16

D_MODEL = 1024
DEPTH = 4
CONV_CH = 512
CONV_KERNEL = 31
HEAD_DIM = 64
N_HEADS = 8
ATTN_WIDTH = 512
MOBA_BLOCK = 256
MOBA_TOPK = 3
NUM_BUCKETS = 32
MAX_DISTANCE = 2048
D_FF = 2816
FFN_CONV = 3
EPS = 1e-6
IN_WIDTH = 2 * CONV_CH + 3 * ATTN_WIDTH + 2 * D_MODEL
Q_COL = 2 * CONV_CH
K_COL = Q_COL + ATTN_WIDTH
V_COL = K_COL + ATTN_WIDTH
GATE_COL = V_COL + ATTN_WIDTH

LANES = 128
SUBLANES = 8
HEAD_PAIRS = N_HEADS // 2
LOG2E = math.log2(math.e)
QK_SCALE = HEAD_DIM ** -0.5
NEG = -30000.0
FAR_BLOCKS = -(-(MAX_DISTANCE + MOBA_BLOCK) // MOBA_BLOCK)
N_BIAS_TILES = FAR_BLOCKS + 1
CONV_HALO = 32
CONV_ROWS = 32
FFN_HALO = SUBLANES
VMEM_LIMIT = 56 * 1024 * 1024

TM = 512
FF_CHUNK = 256
FF_GROUP = 4
KB_GROUP = 4


def _sigmoid(x):
    return 1.0 / (1.0 + jnp.exp(-x))


def _params(*sem):
    return pltpu.CompilerParams(dimension_semantics=sem, vmem_limit_bytes=VMEM_LIMIT)


def _const_spec(shape):
    nd = len(shape)
    return pl.BlockSpec(shape, lambda *_: (0,) * nd, pipeline_mode=pl.Buffered(1))


def _mod_kernel(c_ref, w_ref, b_ref, o_ref):
    c = c_ref[...]
    cond = c * _sigmoid(c)
    o_ref[0] = jnp.dot(cond, w_ref[0], precision=lax.Precision.HIGHEST,
                       preferred_element_type=F32) + b_ref[0]


def _modulation(c, ada_w, ada_b):
    bsz = c.shape[0]
    c8 = jnp.pad(c, ((0, SUBLANES - bsz), (0, 0)))
    tn = 1536
    n = 6 * D_MODEL
    out = pl.pallas_call(
        _mod_kernel,
        grid=(DEPTH, n // tn),
        in_specs=[
            pl.BlockSpec((SUBLANES, D_MODEL), lambda l, j: (0, 0)),
            pl.BlockSpec((1, D_MODEL, tn), lambda l, j: (l, 0, j)),
            pl.BlockSpec((1, 1, tn), lambda l, j: (l, 0, j)),
        ],
        out_specs=pl.BlockSpec((1, SUBLANES, tn), lambda l, j: (l, 0, j)),
        out_shape=jax.ShapeDtypeStruct((DEPTH, SUBLANES, n), F32),
        compiler_params=_params("arbitrary", "arbitrary"),
        name="adaln_modulation",
    )(c8, ada_w, ada_b.reshape(DEPTH, 1, n))
    mod = out[:, :bsz].reshape(DEPTH, bsz, 6, D_MODEL)
    return jnp.pad(mod, ((0, 0), (0, 0), (0, SUBLANES - 6), (0, 0)))


def _mod_rmsnorm(x, g, shift, scale):
    ms = jnp.mean(x * x, axis=-1, keepdims=True)
    return (x * lax.rsqrt(ms + EPS) * g) * (1.0 + scale) + shift


def _inproj_kernel(x_ref, mod_ref, g_ref, w_ref, gq_ref, gk_ref, a_ref, zg_ref, qt_ref, ka_ref, vt_ref, km_ref):
    t = pl.program_id(1)
    mod = mod_ref[0]
    h = _mod_rmsnorm(x_ref[0], g_ref[...], mod[0:1], mod[1:2]).astype(BF16)
    tn = 512

    def proj(col):
        return jnp.dot(h, w_ref[:, col:col + tn], preferred_element_type=F32)

    a_ref[0] = (proj(0) * _sigmoid(proj(CONV_CH))).astype(BF16)
    for n0 in range(0, 2 * D_MODEL, tn):
        zg_ref[0, :, n0:n0 + tn] = proj(GATE_COL + n0).astype(BF16)

    zq, zk, zv = proj(Q_COL), proj(K_COL), proj(V_COL)
    lane = lax.broadcasted_iota(jnp.int32, (MOBA_BLOCK, LANES), 1)
    low = lane < HEAD_DIM

    def head_norm(tv, g):
        t2 = tv * tv
        s0 = jnp.sum(jnp.where(low, t2, 0.0), axis=-1, keepdims=True)
        s1 = jnp.sum(jnp.where(low, 0.0, t2), axis=-1, keepdims=True)
        ms = jnp.where(low, s0, s1) * (1.0 / HEAD_DIM)
        return tv * lax.rsqrt(ms + EPS) * g

    blocks_per_tile = TM // MOBA_BLOCK
    for sb in range(blocks_per_tile):
        r0 = sb * MOBA_BLOCK
        onehot = jnp.where(lane == t * blocks_per_tile + sb, 1.0, 0.0).astype(BF16)
        for p in range(HEAD_PAIRS):
            c0 = p * LANES
            qt_ref[0, p, sb] = head_norm(zq[r0:r0 + MOBA_BLOCK, c0:c0 + LANES], gq_ref[...]).T
            kn = head_norm(zk[r0:r0 + MOBA_BLOCK, c0:c0 + LANES], gk_ref[...])
            km_ref[0, sb, p:p + 1, :] = jnp.mean(kn, axis=0, keepdims=True)
            ka_ref[0, p, r0:r0 + MOBA_BLOCK, :LANES] = kn.astype(BF16)
            ka_ref[0, p, r0:r0 + MOBA_BLOCK, LANES:] = onehot
            vt_ref[0, p, sb] = zv[r0:r0 + MOBA_BLOCK, c0:c0 + LANES].T.astype(BF16)


def _inproj(x, mod_l, g, w, gq2, gk2):
    bsz, seq, _ = x.shape
    nb = seq // MOBA_BLOCK
    bpt = TM // MOBA_BLOCK
    tiles = (bsz, HEAD_PAIRS, nb, LANES, MOBA_BLOCK)
    tile_spec = pl.BlockSpec((1, HEAD_PAIRS, bpt, LANES, MOBA_BLOCK), lambda b, t: (b, 0, t, 0, 0))
    tok = lambda b, t: (b, t, 0)
    return pl.pallas_call(
        _inproj_kernel,
        grid=(bsz, seq // TM),
        in_specs=[
            pl.BlockSpec((1, TM, D_MODEL), tok),
            pl.BlockSpec((1, SUBLANES, D_MODEL), lambda b, t: (b, 0, 0)),
            _const_spec((1, D_MODEL)),
            _const_spec((D_MODEL, IN_WIDTH)),
            _const_spec((1, LANES)),
            _const_spec((1, LANES)),
        ],
        out_specs=[
            pl.BlockSpec((1, TM, CONV_CH), tok),
            pl.BlockSpec((1, TM, 2 * D_MODEL), tok),
            tile_spec,
            pl.BlockSpec((1, HEAD_PAIRS, TM, 2 * LANES), lambda b, t: (b, 0, t, 0)),
            tile_spec,
            pl.BlockSpec((1, bpt, HEAD_PAIRS, LANES), lambda b, t: (b, t, 0, 0)),
        ],
        out_shape=[
            jax.ShapeDtypeStruct((bsz, seq, CONV_CH), BF16),
            jax.ShapeDtypeStruct((bsz, seq, 2 * D_MODEL), BF16),
            jax.ShapeDtypeStruct(tiles, F32),
            jax.ShapeDtypeStruct((bsz, HEAD_PAIRS, seq, 2 * LANES), BF16),
            jax.ShapeDtypeStruct(tiles, BF16),
            jax.ShapeDtypeStruct((bsz, nb, HEAD_PAIRS, LANES), F32),
        ],
        compiler_params=_params("parallel", "arbitrary"),
        name="norm_inproj",
    )(x, mod_l, g, w, gq2, gk2)


def _bias_tile_kernel(rb_ref, o_ref):
    h = pl.program_id(0)
    dblk = pl.program_id(1)
    k = lax.broadcasted_iota(jnp.int32, (MOBA_BLOCK, MOBA_BLOCK), 0)
    q = lax.broadcasted_iota(jnp.int32, (MOBA_BLOCK, MOBA_BLOCK), 1)
    d = dblk * MOBA_BLOCK + q - k
    dist = jnp.maximum(d, 0)
    max_exact = NUM_BUCKETS // 2
    nf = jnp.maximum(dist, 1).astype(F32)
    large = max_exact + (jnp.log(nf / max_exact) / math.log(MAX_DISTANCE / max_exact)
                         * (NUM_BUCKETS - max_exact)).astype(jnp.int32)
    large = jnp.minimum(large, NUM_BUCKETS - 1)
    bucket = jnp.where(dist < max_exact, dist, large)
    val = jnp.zeros((MOBA_BLOCK, MOBA_BLOCK), F32)
    for b in range(NUM_BUCKETS):
        val = jnp.where(bucket == b, rb_ref[h, b], val)
    o_ref[0, 0] = jnp.where(d >= 0, val * LOG2E, NEG)


def _bias_tiles(rel_bias):
    return pl.pallas_call(
        _bias_tile_kernel,
        grid=(N_HEADS, N_BIAS_TILES),
        in_specs=[pl.BlockSpec(memory_space=pltpu.SMEM)],
        out_specs=pl.BlockSpec((1, 1, MOBA_BLOCK, MOBA_BLOCK), lambda h, d: (h, d, 0, 0)),
        out_shape=jax.ShapeDtypeStruct((N_HEADS, N_BIAS_TILES, MOBA_BLOCK, MOBA_BLOCK), F32),
        compiler_params=_params("arbitrary", "arbitrary"),
        name="t5_bias_tiles",
    )(rel_bias)


def _attn_kernel(qt_ref, ka_ref, vt_ref, km_ref, bias_ref, o_ref):
    i = pl.program_id(2)
    nblk = km_ref.shape[2]
    qt = qt_ref[0, 0, 0]
    feat = lax.broadcasted_iota(jnp.int32, (LANES, MOBA_BLOCK), 0)
    blk = lax.broadcasted_iota(jnp.int32, (nblk, MOBA_BLOCK), 0)
    past = blk < i
    q_augs = []
    for hh in range(2):
        hmask = (feat >= hh * HEAD_DIM) & (feat < (hh + 1) * HEAD_DIM)
        qh = jnp.where(hmask, qt, 0.0)
        gate = jnp.dot(km_ref[0, 0], qh, precision=lax.Precision.HIGHEST, preferred_element_type=F32)
        g = jnp.where(past, gate, -jnp.inf)
        sel = blk == i
        for _ in range(MOBA_TOPK):
            mx = jnp.max(g, axis=0, keepdims=True)
            idx = jnp.min(jnp.where(g == mx, blk, nblk), axis=0, keepdims=True)
            hit = blk == idx
            sel = sel | (hit & past)
            g = jnp.where(hit, -jnp.inf, g)
        msk = jnp.concatenate([jnp.where(sel, 0.0, NEG), jnp.zeros((LANES - nblk, MOBA_BLOCK), F32)], axis=0)
        q_augs.append(jnp.concatenate([qh * (QK_SCALE * LOG2E), msk], axis=0).astype(BF16))

    def scores(j):
        start = pl.multiple_of(j * (KB_GROUP * MOBA_BLOCK), KB_GROUP * MOBA_BLOCK)
        ks = [ka_ref[0, 0, pl.ds(start + u * MOBA_BLOCK, MOBA_BLOCK), :] for u in range(KB_GROUP)]
        return [[jnp.dot(k, q_augs[hh], preferred_element_type=F32) for k in ks] for hh in range(2)]

    def softmax_pv(ss, j, carry, near):
        blocks = [KB_GROUP * j + u for u in range(KB_GROUP)]
        vj = jnp.concatenate([vt_ref[0, 0, b] for b in blocks], axis=1)
        ps, stats = [], []
        for hh in range(2):
            m, l, _ = carry[hh]
            if near:
                s = [sv + bias_ref[hh, jnp.clip(i - b, 0, FAR_BLOCKS)] for sv, b in zip(ss[hh], blocks)]
                shift_by = 0.0
            else:
                s = ss[hh]
                shift_by = bias_ref[hh, FAR_BLOCKS, 0:1, 0:1]
            mx = jnp.max(functools.reduce(jnp.maximum, s), axis=0, keepdims=True)
            m_new = jnp.maximum(m, mx + shift_by)
            shift = m_new - shift_by
            alpha = jnp.exp2(m - m_new)
            p = [jnp.exp2(sv - shift) for sv in s]
            psum = jnp.sum(functools.reduce(jnp.add, p), axis=0, keepdims=True)
            stats.append((m_new, alpha * l + psum, alpha))
            ps.append(jnp.concatenate([pv.astype(BF16) for pv in p], axis=0))
        out = []
        for hh in range(2):
            m_new, l, alpha = stats[hh]
            acc = alpha * carry[hh][2] + jnp.dot(vj, ps[hh], preferred_element_type=F32)
            out.append((m_new, l, acc))
        return tuple(out)

    init = tuple((jnp.full((1, MOBA_BLOCK), -jnp.inf, F32), jnp.zeros((1, MOBA_BLOCK), F32),
                  jnp.zeros((LANES, MOBA_BLOCK), F32)) for _ in range(2))

    def step(j, carry, near):
        return softmax_pv(scores(j), j, carry, near)

    n_far = jnp.maximum(i - (FAR_BLOCKS - 1), 0) // KB_GROUP
    carry = lax.fori_loop(0, n_far, functools.partial(step, near=False), init)
    carry = lax.fori_loop(n_far, i // KB_GROUP + 1, functools.partial(step, near=True), carry)
    o0 = carry[0][2] * (1.0 / carry[0][1])
    o1 = carry[1][2] * (1.0 / carry[1][1])
    o_ref[0] = jnp.where(feat < HEAD_DIM, o0, o1).T.astype(BF16)


def _attention(qt, ka, vt, km, bias):
    bsz, _, seq, _ = ka.shape
    nb = seq // MOBA_BLOCK
    return pl.pallas_call(
        _attn_kernel,
        grid=(bsz, HEAD_PAIRS, nb),
        in_specs=[
            pl.BlockSpec((1, 1, 1, LANES, MOBA_BLOCK), lambda b, p, i: (b, p, i, 0, 0)),
            pl.BlockSpec((1, 1, seq, 2 * LANES), lambda b, p, i: (b, p, 0, 0)),
            pl.BlockSpec((1, 1, nb, LANES, MOBA_BLOCK), lambda b, p, i: (b, p, 0, 0, 0)),
            pl.BlockSpec((1, 1, nb, LANES), lambda b, p, i: (b, p, 0, 0)),
            pl.BlockSpec((2, N_BIAS_TILES, MOBA_BLOCK, MOBA_BLOCK), lambda b, p, i: (p, 0, 0, 0)),
        ],
        out_specs=pl.BlockSpec((1, MOBA_BLOCK, LANES), lambda b, p, i: (b, i, p)),
        out_shape=jax.ShapeDtypeStruct((bsz, seq, ATTN_WIDTH), BF16),
        compiler_params=_params("parallel", "parallel", "arbitrary"),
        name="moba_attention",
    )(qt, ka, vt, km, bias)


def _mixout_kernel(a_ref, halo_ref, zg_ref, o_ref, x_ref, mod_ref, cw_ref, cb_ref, lg_ref, lb_ref,
                   wpa_ref, wpb_ref, wo_ref, out_ref, a_ext, a_sh, y_buf):
    t = pl.program_id(1)
    a_ext[CONV_HALO:, :] = a_ref[0].astype(F32)
    a_ext[:CONV_HALO, :] = jnp.where(t > 0, halo_ref[0].astype(F32), 0.0)
    cw = cw_ref[...]
    cb = cb_ref[...]
    first = CONV_HALO - (CONV_KERNEL - 1)
    n_sh = a_sh.shape[1]
    for s in range(1, SUBLANES):
        a_sh[s - 1] = a_ext[s:s + n_sh, :]
    for r0 in range(0, TM, CONV_ROWS):
        acc = jnp.broadcast_to(cb, (CONV_ROWS, CONV_CH))
        for j in range(CONV_KERNEL):
            q, s = divmod(first + j, SUBLANES)
            lo = r0 + SUBLANES * q
            tap = a_ext[lo:lo + CONV_ROWS, :] if s == 0 else a_sh[s - 1, lo:lo + CONV_ROWS, :]
            acc = acc + cw[j:j + 1] * tap
        y_buf[r0:r0 + CONV_ROWS, :] = acc
    y = y_buf[...]
    mu = jnp.mean(y, axis=-1, keepdims=True)
    yc = y - mu
    var = jnp.mean(yc * yc, axis=-1, keepdims=True)
    yn = yc * lax.rsqrt(var + EPS) * lg_ref[...] + lb_ref[...]
    act = (yn * _sigmoid(yn)).astype(BF16)
    ya = jnp.dot(act, wpa_ref[...], preferred_element_type=F32)
    yb = jnp.dot(o_ref[0], wpb_ref[...], preferred_element_type=F32)
    ga = zg_ref[0, :, :D_MODEL].astype(F32)
    gb = zg_ref[0, :, D_MODEL:].astype(F32)
    m = (_sigmoid(ga) * ya + _sigmoid(gb) * yb).astype(BF16)
    y = jnp.dot(m, wo_ref[...], preferred_element_type=F32)
    out_ref[0] = x_ref[0] + mod_ref[0][2:3] * y


def _mixout(a, zg, o, x, mod_l, cw, cb, lg, lb, wpa, wpb, wo):
    bsz, seq, _ = x.shape
    per = TM // CONV_HALO
    tok = lambda b, t: (b, t, 0)
    return pl.pallas_call(
        _mixout_kernel,
        grid=(bsz, seq // TM),
        in_specs=[
            pl.BlockSpec((1, TM, CONV_CH), tok),
            pl.BlockSpec((1, CONV_HALO, CONV_CH), lambda b, t: (b, jnp.maximum(t * per - 1, 0), 0)),
            pl.BlockSpec((1, TM, 2 * D_MODEL), tok),
            pl.BlockSpec((1, TM, ATTN_WIDTH), tok),
            pl.BlockSpec((1, TM, D_MODEL), tok),
            pl.BlockSpec((1, SUBLANES, D_MODEL), lambda b, t: (b, 0, 0)),
            _const_spec((CONV_HALO, CONV_CH)),
            _const_spec((1, CONV_CH)),
            _const_spec((1, CONV_CH)),
            _const_spec((1, CONV_CH)),
            _const_spec((CONV_CH, D_MODEL)),
            _const_spec((ATTN_WIDTH, D_MODEL)),
            _const_spec((D_MODEL, D_MODEL)),
        ],
        out_specs=pl.BlockSpec((1, TM, D_MODEL), tok),
        out_shape=jax.ShapeDtypeStruct((bsz, seq, D_MODEL), F32),
        scratch_shapes=[pltpu.VMEM((TM + CONV_HALO, CONV_CH), F32),
                        pltpu.VMEM((SUBLANES - 1, TM + CONV_HALO - SUBLANES, CONV_CH), F32),
                        pltpu.VMEM((TM, CONV_CH), F32)],
        compiler_params=_params("parallel", "arbitrary"),
        name="mixer_out",
    )(a, a, zg, o, x, mod_l, cw, cb, lg, lb, wpa, wpb, wo)


def _ffn_kernel(x_ref, halo_ref, mod_ref, g_ref, wup_ref, fcw_ref, fcb_ref, wdn_ref, out_ref, act_buf):
    t = pl.program_id(1)
    mod = mod_ref[0]
    x = x_ref[0]
    xe = jnp.concatenate([halo_ref[0], x], axis=0)
    h = _mod_rmsnorm(xe, g_ref[...], mod[3:4], mod[4:5])
    row = lax.broadcasted_iota(jnp.int32, (FFN_HALO + TM, 1), 0)
    h = jnp.where((row >= FFN_HALO) | (t > 0), h, 0.0).astype(BF16)

    def conv(col):
        u = jnp.dot(h, wup_ref[:, col:col + FF_CHUNK], preferred_element_type=F32)
        w = fcw_ref[:, col:col + FF_CHUNK]
        y = w[2:3] * u + w[1:2] * pltpu.roll(u, 1, 0) + w[0:1] * pltpu.roll(u, 2, 0)
        return y[FFN_HALO:] + fcb_ref[:, col:col + FF_CHUNK]

    acc = None
    group = FF_GROUP * FF_CHUNK
    for g0 in range(0, D_FF, group):
        g1 = min(g0 + group, D_FF)
        for c0 in range(g0, g1, FF_CHUNK):
            uv = conv(c0)
            ug = conv(D_FF + c0)
            act_buf[:, c0:c0 + FF_CHUNK] = (ug * _sigmoid(ug) * uv).astype(BF16)
        part = jnp.dot(act_buf[:, g0:g1], wdn_ref[g0:g1, :], preferred_element_type=F32)
        acc = part if acc is None else acc + part
    out_ref[0] = x + mod[5:6] * acc


def _ffn(x, mod_l, g, wup, fcw, fcb, wdn):
    bsz, seq, _ = x.shape
    per = TM // FFN_HALO
    return pl.pallas_call(
        _ffn_kernel,
        grid=(bsz, seq // TM),
        in_specs=[
            pl.BlockSpec((1, TM, D_MODEL), lambda b, t: (b, t, 0)),
            pl.BlockSpec((1, FFN_HALO, D_MODEL), lambda b, t: (b, jnp.maximum(t * per - 1, 0), 0)),
            pl.BlockSpec((1, SUBLANES, D_MODEL), lambda b, t: (b, 0, 0)),
            _const_spec((1, D_MODEL)),
            _const_spec((D_MODEL, 2 * D_FF)),
            _const_spec((SUBLANES, 2 * D_FF)),
            _const_spec((1, 2 * D_FF)),
            _const_spec((D_FF, D_MODEL)),
        ],
        out_specs=pl.BlockSpec((1, TM, D_MODEL), lambda b, t: (b, t, 0)),
        out_shape=jax.ShapeDtypeStruct((bsz, seq, D_MODEL), F32),
        scratch_shapes=[pltpu.VMEM((TM, D_FF), BF16)],
        compiler_params=_params("parallel", "arbitrary"),
        name="conv_ffn",
    )(x, x, mod_l, g, wup, fcw, fcb, wdn)


def kernel(x, c, ada_w, ada_b, norm1_g, w_in, conv_w, conv_b, conv_ln_g, conv_ln_b, q_norm_g, k_norm_g,
           w_pa, w_pb, w_o, norm2_g, w_up, ffn_conv_w, ffn_conv_b, w_down, rel_bias):
    mod = _modulation(c, ada_w, ada_b)
    bias = _bias_tiles(rel_bias)
    cw = jnp.pad(conv_w, ((0, 0), (0, CONV_HALO - CONV_KERNEL), (0, 0)))
    fcw = jnp.pad(ffn_conv_w, ((0, 0), (0, SUBLANES - FFN_CONV), (0, 0)))
    for l in range(DEPTH):
        a, zg, qt, ka, vt, km = _inproj(x, mod[l], norm1_g[l][None], w_in[l].astype(BF16),
                                        jnp.tile(q_norm_g[l], 2)[None], jnp.tile(k_norm_g[l], 2)[None])
        o = _attention(qt, ka, vt, km.transpose(0, 2, 1, 3), bias)
        x = _mixout(a, zg, o, x, mod[l], cw[l], conv_b[l][None], conv_ln_g[l][None], conv_ln_b[l][None],
                    w_pa[l].astype(BF16), w_pb[l].astype(BF16), w_o[l].astype(BF16))
        x = _ffn(x, mod[l], norm2_g[l][None], w_up[l].astype(BF16), fcw[l], ffn_conv_b[l][None],
                 w_down[l].astype(BF16))
    return x
```

```python
import functools
import math

import jax
import jax.numpy as jnp
from jax import lax
from jax.experimental import pallas as pl
from jax.experimental.pallas import tpu as pltpu

F32 = jnp.float32
BF16 = jnp.bfloat16

D_MODEL = 1024
DEPTH = 4
CONV_CH = 512
CONV_KERNEL = 31
HEAD_DIM = 64
N_HEADS = 8
ATTN_WIDTH = 512
MOBA_BLOCK = 256
MOBA_TOPK = 3
NUM_BUCKETS = 32
MAX_DISTANCE = 2048
D_FF = 2816
FFN_CONV = 3
EPS = 1e-6
IN_WIDTH = 2 * CONV_CH + 3 * ATTN_WIDTH + 2 * D_MODEL
Q_COL = 2 * CONV_CH
K_COL = Q_COL + ATTN_WIDTH
V_COL = K_COL + ATTN_WIDTH
GATE_COL = V_COL + ATTN_WIDTH

LANES = 128
SUBLANES = 8
HEAD_PAIRS = N_HEADS // 2
LOG2E = math.log2(math.e)
QK_SCALE = HEAD_DIM ** -0.5
NEG = -30000.0
FAR_BLOCKS = -(-(MAX_DISTANCE + MOBA_BLOCK) // MOBA_BLOCK)
N_BIAS_TILES = FAR_BLOCKS + 1
CONV_HALO = 32
CONV_ROWS = 32
FFN_HALO = SUBLANES
VMEM_LIMIT = 56 * 1024 * 1024

TM = 512
FF_CHUNK = 256
FF_GROUP = 4
KB_GROUP = 4


def _sigmoid(x):
    return 1.0 / (1.0 + jnp.exp(-x))


def _params(*sem):
    return pltpu.CompilerParams(dimension_semantics=sem, vmem_limit_bytes=VMEM_LIMIT)


def _const_spec(shape):
    nd = len(shape)
    return pl.BlockSpec(shape, lambda *_: (0,) * nd, pipeline_mode=pl.Buffered(1))


def _mod_kernel(c_ref, w_ref, b_ref, o_ref):
    c = c_ref[...]
    cond = c * _sigmoid(c)
    o_ref[0] = jnp.dot(cond, w_ref[0], precision=lax.Precision.HIGHEST,
                       preferred_element_type=F32) + b_ref[0]


def _modulation(c, ada_w, ada_b):
    bsz = c.shape[0]
    c8 = jnp.pad(c, ((0, SUBLANES - bsz), (0, 0)))
    tn = 1536
    n = 6 * D_MODEL
    out = pl.pallas_call(
        _mod_kernel,
        grid=(DEPTH, n // tn),
        in_specs=[
            pl.BlockSpec((SUBLANES, D_MODEL), lambda l, j: (0, 0)),
            pl.BlockSpec((1, D_MODEL, tn), lambda l, j: (l, 0, j)),
            pl.BlockSpec((1, 1, tn), lambda l, j: (l, 0, j)),
        ],
        out_specs=pl.BlockSpec((1, SUBLANES, tn), lambda l, j: (l, 0, j)),
        out_shape=jax.ShapeDtypeStruct((DEPTH, SUBLANES, n), F32),
        compiler_params=_params("arbitrary", "arbitrary"),
        name="adaln_modulation",
    )(c8, ada_w, ada_b.reshape(DEPTH, 1, n))
    mod = out[:, :bsz].reshape(DEPTH, bsz, 6, D_MODEL)
    return jnp.pad(mod, ((0, 0), (0, 0), (0, SUBLANES - 6), (0, 0)))


def _mod_rmsnorm(x, g, shift, scale):
    ms = jnp.mean(x * x, axis=-1, keepdims=True)
    return (x * lax.rsqrt(ms + EPS) * g) * (1.0 + scale) + shift


def _inproj_kernel(x_ref, mod_ref, g_ref, w_ref, gq_ref, gk_ref, a_ref, zg_ref, qt_ref, ka_ref, vt_ref, km_ref):
    t = pl.program_id(1)
    mod = mod_ref[0]
    h = _mod_rmsnorm(x_ref[0], g_ref[...], mod[0:1], mod[1:2]).astype(BF16)
    tn = 512

    def proj(col):
        return jnp.dot(h, w_ref[:, col:col + tn], preferred_element_type=F32)

    a_ref[0] = (proj(0) * _sigmoid(proj(CONV_CH))).astype(BF16)
    for n0 in range(0, 2 * D_MODEL, tn):
        zg_ref[0, :, n0:n0 + tn] = proj(GATE_COL + n0).astype(BF16)

    zq, zk, zv = proj(Q_COL), proj(K_COL), proj(V_COL)
    lane = lax.broadcasted_iota(jnp.int32, (MOBA_BLOCK, LANES), 1)
    low = lane < HEAD_DIM

    def head_norm(tv, g):
        t2 = tv * tv
        s0 = jnp.sum(jnp.where(low, t2, 0.0), axis=-1, keepdims=True)
        s1 = jnp.sum(jnp.where(low, 0.0, t2), axis=-1, keepdims=True)
        ms = jnp.where(low, s0, s1) * (1.0 / HEAD_DIM)
        return tv * lax.rsqrt(ms + EPS) * g

    blocks_per_tile = TM // MOBA_BLOCK
    for sb in range(blocks_per_tile):
        r0 = sb * MOBA_BLOCK
        onehot = jnp.where(lane == t * blocks_per_tile + sb, 1.0, 0.0).astype(BF16)
        for p in range(HEAD_PAIRS):
            c0 = p * LANES
            qt_ref[0, p, sb] = head_norm(zq[r0:r0 + MOBA_BLOCK, c0:c0 + LANES], gq_ref[...]).T
            kn = head_norm(zk[r0:r0 + MOBA_BLOCK, c0:c0 + LANES], gk_ref[...])
            km_ref[0, sb, p:p + 1, :] = jnp.mean(kn, axis=0, keepdims=True)
            ka_ref[0, p, r0:r0 + MOBA_BLOCK, :LANES] = kn.astype(BF16)
            ka_ref[0, p, r0:r0 + MOBA_BLOCK, LANES:] = onehot
            vt_ref[0, p, sb] = zv[r0:r0 + MOBA_BLOCK, c0:c0 + LANES].T.astype(BF16)


def _inproj(x, mod_l, g, w, gq2, gk2):
    bsz, seq, _ = x.shape
    nb = seq // MOBA_BLOCK
    bpt = TM // MOBA_BLOCK
    tiles = (bsz, HEAD_PAIRS, nb, LANES, MOBA_BLOCK)
    tile_spec = pl.BlockSpec((1, HEAD_PAIRS, bpt, LANES, MOBA_BLOCK), lambda b, t: (b, 0, t, 0, 0))
    tok = lambda b, t: (b, t, 0)
    return pl.pallas_call(
        _inproj_kernel,
        grid=(bsz, seq // TM),
        in_specs=[
            pl.BlockSpec((1, TM, D_MODEL), tok),
            pl.BlockSpec((1, SUBLANES, D_MODEL), lambda b, t: (b, 0, 0)),
            _const_spec((1, D_MODEL)),
            _const_spec((D_MODEL, IN_WIDTH)),
            _const_spec((1, LANES)),
            _const_spec((1, LANES)),
        ],
        out_specs=[
            pl.BlockSpec((1, TM, CONV_CH), tok),
            pl.BlockSpec((1, TM, 2 * D_MODEL), tok),
            tile_spec,
            pl.BlockSpec((1, HEAD_PAIRS, TM, 2 * LANES), lambda b, t: (b, 0, t, 0)),
            tile_spec,
            pl.BlockSpec((1, bpt, HEAD_PAIRS, LANES), lambda b, t: (b, t, 0, 0)),
        ],
        out_shape=[
            jax.ShapeDtypeStruct((bsz, seq, CONV_CH), BF16),
            jax.ShapeDtypeStruct((bsz, seq, 2 * D_MODEL), BF16),
            jax.ShapeDtypeStruct(tiles, F32),
            jax.ShapeDtypeStruct((bsz, HEAD_PAIRS, seq, 2 * LANES), BF16),
            jax.ShapeDtypeStruct(tiles, BF16),
            jax.ShapeDtypeStruct((bsz, nb, HEAD_PAIRS, LANES), F32),
        ],
        compiler_params=_params("parallel", "arbitrary"),
        name="norm_inproj",
    )(x, mod_l, g, w, gq2, gk2)


def _bias_tile_kernel(rb_ref, o_ref):
    h = pl.program_id(0)
    dblk = pl.program_id(1)
    k = lax.broadcasted_iota(jnp.int32, (MOBA_BLOCK, MOBA_BLOCK), 0)
    q = lax.broadcasted_iota(jnp.int32, (MOBA_BLOCK, MOBA_BLOCK), 1)
    d = dblk * MOBA_BLOCK + q - k
    dist = jnp.maximum(d, 0)
    max_exact = NUM_BUCKETS // 2
    nf = jnp.maximum(dist, 1).astype(F32)
    large = max_exact + (jnp.log(nf / max_exact) / math.log(MAX_DISTANCE / max_exact)
                         * (NUM_BUCKETS - max_exact)).astype(jnp.int32)
    large = jnp.minimum(large, NUM_BUCKETS - 1)
    bucket = jnp.where(dist < max_exact, dist, large)
    val = jnp.zeros((MOBA_BLOCK, MOBA_BLOCK), F32)
    for b in range(NUM_BUCKETS):
        val = jnp.where(bucket == b, rb_ref[h, b], val)
    o_ref[0, 0] = jnp.where(d >= 0, val * LOG2E, NEG)


def _bias_tiles(rel_bias):
    return pl.pallas_call(
        _bias_tile_kernel,
        grid=(N_HEADS, N_BIAS_TILES),
        in_specs=[pl.BlockSpec(memory_space=pltpu.SMEM)],
        out_specs=pl.BlockSpec((1, 1, MOBA_BLOCK, MOBA_BLOCK), lambda h, d: (h, d, 0, 0)),
        out_shape=jax.ShapeDtypeStruct((N_HEADS, N_BIAS_TILES, MOBA_BLOCK, MOBA_BLOCK), F32),
        compiler_params=_params("arbitrary", "arbitrary"),
        name="t5_bias_tiles",
    )(rel_bias)


def _attn_kernel(qt_ref, ka_ref, vt_ref, km_ref, bias_ref, o_ref, *, n_groups):
    i = (n_groups - 1) * KB_GROUP + pl.program_id(2)
    nblk = km_ref.shape[2]
    qt = qt_ref[0, 0, 0]
    feat = lax.broadcasted_iota(jnp.int32, (LANES, MOBA_BLOCK), 0)
    blk = lax.broadcasted_iota(jnp.int32, (nblk, MOBA_BLOCK), 0)
    past = blk < i
    q_augs = []
    for hh in range(2):
        hmask = (feat >= hh * HEAD_DIM) & (feat < (hh + 1) * HEAD_DIM)
        qh = jnp.where(hmask, qt, 0.0)
        gate = jnp.dot(km_ref[0, 0], qh, precision=lax.Precision.HIGHEST, preferred_element_type=F32)
        g = jnp.where(past, gate, -jnp.inf)
        sel = blk == i
        for _ in range(MOBA_TOPK):
            mx = jnp.max(g, axis=0, keepdims=True)
            idx = jnp.min(jnp.where(g == mx, blk, nblk), axis=0, keepdims=True)
            hit = blk == idx
            sel = sel | (hit & past)
            g = jnp.where(hit, -jnp.inf, g)
        msk = jnp.concatenate([jnp.where(sel, 0.0, NEG), jnp.zeros((LANES - nblk, MOBA_BLOCK), F32)], axis=0)
        q_augs.append(jnp.concatenate([qh * (QK_SCALE * LOG2E), msk], axis=0).astype(BF16))

    def scores(j):
        ks = [ka_ref[0, 0, (KB_GROUP * j + u) * MOBA_BLOCK:(KB_GROUP * j + u + 1) * MOBA_BLOCK, :]
              for u in range(KB_GROUP)]
        return [[jnp.dot(k, q_augs[hh], preferred_element_type=F32) for k in ks] for hh in range(2)]

    def softmax_pv(ss, j, carry, near):
        blocks = [KB_GROUP * j + u for u in range(KB_GROUP)]
        vj = jnp.concatenate([vt_ref[0, 0, b] for b in blocks], axis=1)
        ps, stats = [], []
        for hh in range(2):
            m, l, _ = carry[hh]
            if near:
                s = [sv + bias_ref[hh, jnp.clip(i - b, 0, FAR_BLOCKS)] for sv, b in zip(ss[hh], blocks)]
                shift_by = 0.0
            else:
                s = ss[hh]
                shift_by = bias_ref[hh, FAR_BLOCKS, 0:1, 0:1]
            mx = jnp.max(functools.reduce(jnp.maximum, s), axis=0, keepdims=True)
            m_new = jnp.maximum(m, mx + shift_by)
            shift = m_new - shift_by
            alpha = jnp.exp2(m - m_new)
            p = [jnp.exp2(sv - shift) for sv in s]
            psum = jnp.sum(functools.reduce(jnp.add, p), axis=0, keepdims=True)
            stats.append((m_new, alpha * l + psum, alpha))
            ps.append(jnp.concatenate([pv.astype(BF16) for pv in p], axis=0))
        out = []
        for hh in range(2):
            m_new, l, alpha = stats[hh]
            acc = alpha * carry[hh][2] + jnp.dot(vj, ps[hh], preferred_element_type=F32)
            out.append((m_new, l, acc))
        return tuple(out)

    carry = tuple((jnp.full((1, MOBA_BLOCK), -jnp.inf, F32), jnp.zeros((1, MOBA_BLOCK), F32),
                   jnp.zeros((LANES, MOBA_BLOCK), F32)) for _ in range(2))
    n_far = max(((n_groups - 1) * KB_GROUP - (FAR_BLOCKS - 1)) // KB_GROUP, 0)
    ss = scores(0)
    for j in range(n_groups):
        ahead = scores(j + 1) if j + 1 < n_groups else None
        carry = softmax_pv(ss, j, carry, near=j >= n_far)
        ss = ahead
    o0 = carry[0][2] * (1.0 / carry[0][1])
    o1 = carry[1][2] * (1.0 / carry[1][1])
    o_ref[0] = jnp.where(feat < HEAD_DIM, o0, o1).T.astype(BF16)


def _attention(qt, ka, vt, km, bias):
    bsz, _, seq, _ = ka.shape
    nb = seq // MOBA_BLOCK
    outs = []
    for n_groups in range(1, nb // KB_GROUP + 1):
        first = (n_groups - 1) * KB_GROUP
        keys = n_groups * KB_GROUP
        outs.append(pl.pallas_call(
            functools.partial(_attn_kernel, n_groups=n_groups),
            grid=(bsz, HEAD_PAIRS, KB_GROUP),
            in_specs=[
                pl.BlockSpec((1, 1, 1, LANES, MOBA_BLOCK), lambda b, p, r, first=first: (b, p, first + r, 0, 0)),
                pl.BlockSpec((1, 1, keys * MOBA_BLOCK, 2 * LANES), lambda b, p, r: (b, p, 0, 0)),
                pl.BlockSpec((1, 1, keys, LANES, MOBA_BLOCK), lambda b, p, r: (b, p, 0, 0, 0)),
                pl.BlockSpec((1, 1, nb, LANES), lambda b, p, r: (b, p, 0, 0)),
                pl.BlockSpec((2, N_BIAS_TILES, MOBA_BLOCK, MOBA_BLOCK), lambda b, p, r: (p, 0, 0, 0)),
            ],
            out_specs=pl.BlockSpec((1, MOBA_BLOCK, LANES), lambda b, p, r: (b, r, p)),
            out_shape=jax.ShapeDtypeStruct((bsz, KB_GROUP * MOBA_BLOCK, ATTN_WIDTH), BF16),
            compiler_params=_params("parallel", "parallel", "arbitrary"),
            name=f"moba_attention_g{n_groups}",
        )(qt, ka, vt, km, bias))
    return jnp.concatenate(outs, axis=1)


def _mixout_kernel(a_ref, halo_ref, zg_ref, o_ref, x_ref, mod_ref, cw_ref, cb_ref, lg_ref, lb_ref,
                   wpa_ref, wpb_ref, wo_ref, out_ref, a_ext, a_sh, y_buf):
    t = pl.program_id(1)
    a_ext[CONV_HALO:, :] = a_ref[0].astype(F32)
    a_ext[:CONV_HALO, :] = jnp.where(t > 0, halo_ref[0].astype(F32), 0.0)
    cw = cw_ref[...]
    cb = cb_ref[...]
    first = CONV_HALO - (CONV_KERNEL - 1)
    n_sh = a_sh.shape[1]
    for s in range(1, SUBLANES):
        a_sh[s - 1] = a_ext[s:s + n_sh, :]
    for r0 in range(0, TM, CONV_ROWS):
        acc = jnp.broadcast_to(cb, (CONV_ROWS, CONV_CH))
        for j in range(CONV_KERNEL):
            q, s = divmod(first + j, SUBLANES)
            lo = r0 + SUBLANES * q
            tap = a_ext[lo:lo + CONV_ROWS, :] if s == 0 else a_sh[s - 1, lo:lo + CONV_ROWS, :]
            acc = acc + cw[j:j + 1] * tap
        y_buf[r0:r0 + CONV_ROWS, :] = acc
    y = y_buf[...]
    mu = jnp.mean(y, axis=-1, keepdims=True)
    yc = y - mu
    var = jnp.mean(yc * yc, axis=-1, keepdims=True)
    yn = yc * lax.rsqrt(var + EPS) * lg_ref[...] + lb_ref[...]
    act = (yn * _sigmoid(yn)).astype(BF16)
    ya = jnp.dot(act, wpa_ref[...], preferred_element_type=F32)
    yb = jnp.dot(o_ref[0], wpb_ref[...], preferred_element_type=F32)
    ga = zg_ref[0, :, :D_MODEL].astype(F32)
    gb = zg_ref[0, :, D_MODEL:].astype(F32)
    m = (_sigmoid(ga) * ya + _sigmoid(gb) * yb).astype(BF16)
    y = jnp.dot(m, wo_ref[...], preferred_element_type=F32)
    out_ref[0] = x_ref[0] + mod_ref[0][2:3] * y


def _mixout(a, zg, o, x, mod_l, cw, cb, lg, lb, wpa, wpb, wo):
    bsz, seq, _ = x.shape
    per = TM // CONV_HALO
    tok = lambda b, t: (b, t, 0)
    return pl.pallas_call(
        _mixout_kernel,
        grid=(bsz, seq // TM),
        in_specs=[
            pl.BlockSpec((1, TM, CONV_CH), tok),
            pl.BlockSpec((1, CONV_HALO, CONV_CH), lambda b, t: (b, jnp.maximum(t * per - 1, 0), 0)),
            pl.BlockSpec((1, TM, 2 * D_MODEL), tok),
            pl.BlockSpec((1, TM, ATTN_WIDTH), tok),
            pl.BlockSpec((1, TM, D_MODEL), tok),
            pl.BlockSpec((1, SUBLANES, D_MODEL), lambda b, t: (b, 0, 0)),
            _const_spec((CONV_HALO, CONV_CH)),
            _const_spec((1, CONV_CH)),
            _const_spec((1, CONV_CH)),
            _const_spec((1, CONV_CH)),
            _const_spec((CONV_CH, D_MODEL)),
            _const_spec((ATTN_WIDTH, D_MODEL)),
            _const_spec((D_MODEL, D_MODEL)),
        ],
        out_specs=pl.BlockSpec((1, TM, D_MODEL), tok),
        out_shape=jax.ShapeDtypeStruct((bsz, seq, D_MODEL), F32),
        scratch_shapes=[pltpu.VMEM((TM + CONV_HALO, CONV_CH), F32),
                        pltpu.VMEM((SUBLANES - 1, TM + CONV_HALO - SUBLANES, CONV_CH), F32),
                        pltpu.VMEM((TM, CONV_CH), F32)],
        compiler_params=_params("parallel", "arbitrary"),
        name="mixer_out",
    )(a, a, zg, o, x, mod_l, cw, cb, lg, lb, wpa, wpb, wo)


def _ffn_kernel(x_ref, halo_ref, mod_ref, g_ref, wup_ref, fcw_ref, fcb_ref, wdn_ref, out_ref, act_buf):
    t = pl.program_id(1)
    mod = mod_ref[0]
    x = x_ref[0]
    xe = jnp.concatenate([halo_ref[0], x], axis=0)
    h = _mod_rmsnorm(xe, g_ref[...], mod[3:4], mod[4:5])
    row = lax.broadcasted_iota(jnp.int32, (FFN_HALO + TM, 1), 0)
    h = jnp.where((row >= FFN_HALO) | (t > 0), h, 0.0).astype(BF16)

    def conv(col):
        u = jnp.dot(h, wup_ref[:, col:col + FF_CHUNK], preferred_element_type=F32)
        w = fcw_ref[:, col:col + FF_CHUNK]
        y = w[2:3] * u + w[1:2] * pltpu.roll(u, 1, 0) + w[0:1] * pltpu.roll(u, 2, 0)
        return y[FFN_HALO:] + fcb_ref[:, col:col + FF_CHUNK]

    acc = None
    group = FF_GROUP * FF_CHUNK
    for g0 in range(0, D_FF, group):
        g1 = min(g0 + group, D_FF)
        for c0 in range(g0, g1, FF_CHUNK):
            uv = conv(c0)
            ug = conv(D_FF + c0)
            act_buf[:, c0:c0 + FF_CHUNK] = (ug * _sigmoid(ug) * uv).astype(BF16)
        part = jnp.dot(act_buf[:, g0:g1], wdn_ref[g0:g1, :], preferred_element_type=F32)
        acc = part if acc is None else acc + part
    out_ref[0] = x + mod[5:6] * acc


def _ffn(x, mod_l, g, wup, fcw, fcb, wdn):
    bsz, seq, _ = x.shape
    per = TM // FFN_HALO
    return pl.pallas_call(
        _ffn_kernel,
        grid=(bsz, seq // TM),
        in_specs=[
            pl.BlockSpec((1, TM, D_MODEL), lambda b, t: (b, t, 0)),
            pl.BlockSpec((1, FFN_HALO, D_MODEL), lambda b, t: (b, jnp.maximum(t * per - 1, 0), 0)),
            pl.BlockSpec((1, SUBLANES, D_MODEL), lambda b, t: (b, 0, 0)),
            _const_spec((1, D_MODEL)),
            _const_spec((D_MODEL, 2 * D_FF)),
            _const_spec((SUBLANES, 2 * D_FF)),
            _const_spec((1, 2 * D_FF)),
            _const_spec((D_FF, D_MODEL)),
        ],
        out_specs=pl.BlockSpec((1, TM, D_MODEL), lambda b, t: (b, t, 0)),
        out_shape=jax.ShapeDtypeStruct((bsz, seq, D_MODEL), F32),
        scratch_shapes=[pltpu.VMEM((TM, D_FF), BF16)],
        compiler_params=_params("parallel", "arbitrary"),
        name="conv_ffn",
    )(x, x, mod_l, g, wup, fcw, fcb, wdn)


def kernel(x, c, ada_w, ada_b, norm1_g, w_in, conv_w, conv_b, conv_ln_g, conv_ln_b, q_norm_g, k_norm_g,
           w_pa, w_pb, w_o, norm2_g, w_up, ffn_conv_w, ffn_conv_b, w_down, rel_bias):
    mod = _modulation(c, ada_w, ada_b)
    bias = _bias_tiles(rel_bias)
    cw = jnp.pad(conv_w, ((0, 0), (0, CONV_HALO - CONV_KERNEL), (0, 0)))
    fcw = jnp.pad(ffn_conv_w, ((0, 0), (0, SUBLANES - FFN_CONV), (0, 0)))
    for l in range(DEPTH):
        a, zg, qt, ka, vt, km = _inproj(x, mod[l], norm1_g[l][None], w_in[l].astype(BF16),
                                        jnp.tile(q_norm_g[l], 2)[None], jnp.tile(k_norm_g[l], 2)[None])
        o = _attention(qt, ka, vt, km.transpose(0, 2, 1, 3), bias)
        x = _mixout(a, zg, o, x, mod[l], cw[l], conv_b[l][None], conv_ln_g[l][None], conv_ln_b[l][None],
                    w_pa[l].astype(BF16), w_pb[l].astype(BF16), w_o[l].astype(BF16))
        x = _ffn(x, mod[l], norm2_g[l][None], w_up[l].astype(BF16), fcw[l], ffn_conv_b[l][None],
                 w_down[l].astype(BF16))
    return x
```

```python
import functools
import math

import jax
import jax.numpy as jnp
from jax import lax
from jax.experimental import pallas as pl
from jax.experimental.pallas import tpu as pltpu

F32 = jnp.float32
BF16 = jnp.bfloat16

D_MODEL = 1024
DEPTH = 4
CONV_CH = 512
CONV_KERNEL = 31
HEAD_DIM = 64
N_HEADS = 8
ATTN_WIDTH = 512
MOBA_BLOCK = 256
MOBA_TOPK = 3
NUM_BUCKETS = 32
MAX_DISTANCE = 2048
D_FF = 2816
FFN_CONV = 3
EPS = 1e-6
IN_WIDTH = 2 * CONV_CH + 3 * ATTN_WIDTH + 2 * D_MODEL
Q_COL = 2 * CONV_CH
K_COL = Q_COL + ATTN_WIDTH
V_COL = K_COL + ATTN_WIDTH
GATE_COL = V_COL + ATTN_WIDTH

LANES = 128
SUBLANES = 8
HEAD_PAIRS = N_HEADS // 2
LOG2E = math.log2(math.e)
QK_SCALE = HEAD_DIM ** -0.5
NEG = -30000.0
V_ROWS = HEAD_DIM + 16
FAR_BLOCKS = -(-(MAX_DISTANCE + MOBA_BLOCK) // MOBA_BLOCK)
N_BIAS_TILES = FAR_BLOCKS + 1
CONV_HALO = 32
CONV_ROWS = 32
FFN_HALO = SUBLANES
VMEM_LIMIT = 56 * 1024 * 1024

TM = 512
FF_CHUNK = 256
FF_GROUP = 4
KB_GROUP = 4


def _sigmoid(x):
    return 1.0 / (1.0 + jnp.exp(-x))


def _params(*sem):
    return pltpu.CompilerParams(dimension_semantics=sem, vmem_limit_bytes=VMEM_LIMIT)


def _const_spec(shape):
    nd = len(shape)
    return pl.BlockSpec(shape, lambda *_: (0,) * nd, pipeline_mode=pl.Buffered(1))


def _mod_kernel(c_ref, w_ref, b_ref, o_ref):
    c = c_ref[...]
    cond = c * _sigmoid(c)
    o_ref[0] = jnp.dot(cond, w_ref[0], precision=lax.Precision.HIGHEST,
                       preferred_element_type=F32) + b_ref[0]


def _modulation(c, ada_w, ada_b):
    bsz = c.shape[0]
    c8 = jnp.pad(c, ((0, SUBLANES - bsz), (0, 0)))
    tn = 1536
    n = 6 * D_MODEL
    out = pl.pallas_call(
        _mod_kernel,
        grid=(DEPTH, n // tn),
        in_specs=[
            pl.BlockSpec((SUBLANES, D_MODEL), lambda l, j: (0, 0)),
            pl.BlockSpec((1, D_MODEL, tn), lambda l, j: (l, 0, j)),
            pl.BlockSpec((1, 1, tn), lambda l, j: (l, 0, j)),
        ],
        out_specs=pl.BlockSpec((1, SUBLANES, tn), lambda l, j: (l, 0, j)),
        out_shape=jax.ShapeDtypeStruct((DEPTH, SUBLANES, n), F32),
        compiler_params=_params("arbitrary", "arbitrary"),
        name="adaln_modulation",
    )(c8, ada_w, ada_b.reshape(DEPTH, 1, n))
    mod = out[:, :bsz].reshape(DEPTH, bsz, 6, D_MODEL)
    return jnp.pad(mod, ((0, 0), (0, 0), (0, SUBLANES - 6), (0, 0)))


def _mod_rmsnorm(x, g, shift, scale):
    ms = jnp.mean(x * x, axis=-1, keepdims=True)
    return (x * lax.rsqrt(ms + EPS) * g) * (1.0 + scale) + shift


def _inproj_kernel(x_ref, mod_ref, g_ref, w_ref, gq_ref, gk_ref, a_ref, zg_ref, qt_ref, ka_ref, vt_ref, km_ref):
    t = pl.program_id(1)
    mod = mod_ref[0]
    h = _mod_rmsnorm(x_ref[0], g_ref[...], mod[0:1], mod[1:2]).astype(BF16)
    tn = 512

    def proj(col):
        return jnp.dot(h, w_ref[:, col:col + tn], preferred_element_type=F32)

    a_ref[0] = (proj(0) * _sigmoid(proj(CONV_CH))).astype(BF16)
    for n0 in range(0, 2 * D_MODEL, tn):
        zg_ref[0, :, n0:n0 + tn] = proj(GATE_COL + n0).astype(BF16)

    zq, zk, zv = proj(Q_COL), proj(K_COL), proj(V_COL)
    lane = lax.broadcasted_iota(jnp.int32, (MOBA_BLOCK, LANES), 1)
    low = lane < HEAD_DIM

    def head_norm(tv, g):
        t2 = tv * tv
        s0 = jnp.sum(jnp.where(low, t2, 0.0), axis=-1, keepdims=True)
        s1 = jnp.sum(jnp.where(low, 0.0, t2), axis=-1, keepdims=True)
        ms = jnp.where(low, s0, s1) * (1.0 / HEAD_DIM)
        return tv * lax.rsqrt(ms + EPS) * g

    blocks_per_tile = TM // MOBA_BLOCK
    ones_rows = jnp.where(lax.broadcasted_iota(jnp.int32, (V_ROWS - HEAD_DIM, MOBA_BLOCK), 0) == 0, 1.0, 0.0)
    for sb in range(blocks_per_tile):
        r0 = sb * MOBA_BLOCK
        onehot = jnp.where(lane - HEAD_DIM == t * blocks_per_tile + sb, 1.0, 0.0)
        for p in range(HEAD_PAIRS):
            c0 = p * LANES
            qt_ref[0, p, sb] = head_norm(zq[r0:r0 + MOBA_BLOCK, c0:c0 + LANES], gq_ref[...]).T
            kn = head_norm(zk[r0:r0 + MOBA_BLOCK, c0:c0 + LANES], gk_ref[...])
            km_ref[0, sb, p:p + 1, :] = jnp.mean(kn, axis=0, keepdims=True)
            vt = zv[r0:r0 + MOBA_BLOCK, c0:c0 + LANES].T
            for hh, kh in enumerate((kn, pltpu.roll(kn, HEAD_DIM, 1))):
                ka_ref[0, 2 * p + hh, r0:r0 + MOBA_BLOCK, :] = jnp.where(low, kh, onehot).astype(BF16)
                vh = vt[hh * HEAD_DIM:(hh + 1) * HEAD_DIM]
                vt_ref[0, 2 * p + hh, sb] = jnp.concatenate([vh, ones_rows], axis=0).astype(BF16)


def _inproj(x, mod_l, g, w, gq2, gk2):
    bsz, seq, _ = x.shape
    nb = seq // MOBA_BLOCK
    bpt = TM // MOBA_BLOCK
    tiles = (bsz, HEAD_PAIRS, nb, LANES, MOBA_BLOCK)
    tile_spec = pl.BlockSpec((1, HEAD_PAIRS, bpt, LANES, MOBA_BLOCK), lambda b, t: (b, 0, t, 0, 0))
    tok = lambda b, t: (b, t, 0)
    return pl.pallas_call(
        _inproj_kernel,
        grid=(bsz, seq // TM),
        in_specs=[
            pl.BlockSpec((1, TM, D_MODEL), tok),
            pl.BlockSpec((1, SUBLANES, D_MODEL), lambda b, t: (b, 0, 0)),
            _const_spec((1, D_MODEL)),
            _const_spec((D_MODEL, IN_WIDTH)),
            _const_spec((1, LANES)),
            _const_spec((1, LANES)),
        ],
        out_specs=[
            pl.BlockSpec((1, TM, CONV_CH), tok),
            pl.BlockSpec((1, TM, 2 * D_MODEL), tok),
            tile_spec,
            pl.BlockSpec((1, N_HEADS, TM, LANES), lambda b, t: (b, 0, t, 0)),
            pl.BlockSpec((1, N_HEADS, bpt, V_ROWS, MOBA_BLOCK), lambda b, t: (b, 0, t, 0, 0)),
            pl.BlockSpec((1, bpt, HEAD_PAIRS, LANES), lambda b, t: (b, t, 0, 0)),
        ],
        out_shape=[
            jax.ShapeDtypeStruct((bsz, seq, CONV_CH), BF16),
            jax.ShapeDtypeStruct((bsz, seq, 2 * D_MODEL), BF16),
            jax.ShapeDtypeStruct(tiles, F32),
            jax.ShapeDtypeStruct((bsz, N_HEADS, seq, LANES), BF16),
            jax.ShapeDtypeStruct((bsz, N_HEADS, nb, V_ROWS, MOBA_BLOCK), BF16),
            jax.ShapeDtypeStruct((bsz, nb, HEAD_PAIRS, LANES), F32),
        ],
        compiler_params=_params("parallel", "arbitrary"),
        name="norm_inproj",
    )(x, mod_l, g, w, gq2, gk2)


def _bias_tile_kernel(rb_ref, o_ref):
    h = pl.program_id(0)
    dblk = pl.program_id(1)
    k = lax.broadcasted_iota(jnp.int32, (MOBA_BLOCK, MOBA_BLOCK), 0)
    q = lax.broadcasted_iota(jnp.int32, (MOBA_BLOCK, MOBA_BLOCK), 1)
    d = dblk * MOBA_BLOCK + q - k
    dist = jnp.maximum(d, 0)
    max_exact = NUM_BUCKETS // 2
    nf = jnp.maximum(dist, 1).astype(F32)
    large = max_exact + (jnp.log(nf / max_exact) / math.log(MAX_DISTANCE / max_exact)
                         * (NUM_BUCKETS - max_exact)).astype(jnp.int32)
    large = jnp.minimum(large, NUM_BUCKETS - 1)
    bucket = jnp.where(dist < max_exact, dist, large)
    val = jnp.zeros((MOBA_BLOCK, MOBA_BLOCK), F32)
    for b in range(NUM_BUCKETS):
        val = jnp.where(bucket == b, rb_ref[h, b], val)
    o_ref[0, 0] = jnp.where(d >= 0, val * LOG2E, NEG)


def _bias_tiles(rel_bias):
    return pl.pallas_call(
        _bias_tile_kernel,
        grid=(N_HEADS, N_BIAS_TILES),
        in_specs=[pl.BlockSpec(memory_space=pltpu.SMEM)],
        out_specs=pl.BlockSpec((1, 1, MOBA_BLOCK, MOBA_BLOCK), lambda h, d: (h, d, 0, 0)),
        out_shape=jax.ShapeDtypeStruct((N_HEADS, N_BIAS_TILES, MOBA_BLOCK, MOBA_BLOCK), F32),
        compiler_params=_params("arbitrary", "arbitrary"),
        name="t5_bias_tiles",
    )(rel_bias)


def _attn_kernel(qt_ref, ka_ref, vt_ref, km_ref, bias_ref, o_ref, *, n_groups):
    i = (n_groups - 1) * KB_GROUP + pl.program_id(2)
    nblk = km_ref.shape[2]
    qt = qt_ref[0, 0, 0]
    feat = lax.broadcasted_iota(jnp.int32, (LANES, MOBA_BLOCK), 0)
    blk = lax.broadcasted_iota(jnp.int32, (nblk, MOBA_BLOCK), 0)
    past = blk < i
    q_augs = []
    for hh in range(2):
        hmask = (feat >= hh * HEAD_DIM) & (feat < (hh + 1) * HEAD_DIM)
        qh = jnp.where(hmask, qt, 0.0)
        gate = jnp.dot(km_ref[0, 0], qh, precision=lax.Precision.HIGHEST, preferred_element_type=F32)
        g = jnp.where(past, gate, -jnp.inf)
        sel = blk == i
        for _ in range(MOBA_TOPK):
            mx = jnp.max(g, axis=0, keepdims=True)
            idx = jnp.min(jnp.where(g == mx, blk, nblk), axis=0, keepdims=True)
            hit = blk == idx
            sel = sel | (hit & past)
            g = jnp.where(hit, -jnp.inf, g)
        q_augs.append(jnp.concatenate(
            [qt[hh * HEAD_DIM:(hh + 1) * HEAD_DIM] * (QK_SCALE * LOG2E), jnp.where(sel, 0.0, NEG),
             jnp.zeros((LANES - HEAD_DIM - nblk, MOBA_BLOCK), F32)], axis=0).astype(BF16))

    def scores(j):
        rows = [slice((KB_GROUP * j + u) * MOBA_BLOCK, (KB_GROUP * j + u + 1) * MOBA_BLOCK) for u in range(KB_GROUP)]
        return [[jnp.dot(ka_ref[0, hh, r, :], q_augs[hh], preferred_element_type=F32) for r in rows]
                for hh in range(2)]

    def softmax_pv(ss, j, carry, near):
        blocks = [KB_GROUP * j + u for u in range(KB_GROUP)]
        ps, stats = [], []
        for hh in range(2):
            m, _ = carry[hh]
            if near:
                s = [sv + bias_ref[hh, jnp.clip(i - b, 0, FAR_BLOCKS)] for sv, b in zip(ss[hh], blocks)]
                shift_by = 0.0
            else:
                s = ss[hh]
                shift_by = bias_ref[hh, FAR_BLOCKS, 0:1, 0:1]
            mx = jnp.max(functools.reduce(jnp.maximum, s), axis=0, keepdims=True)
            m_new = jnp.maximum(m, mx + shift_by)
            shift = m_new - shift_by
            alpha = jnp.exp2(m - m_new)
            stats.append((m_new, alpha))
            ps.append(jnp.concatenate([jnp.exp2(sv - shift).astype(BF16) for sv in s], axis=0))
        out = []
        for hh in range(2):
            m_new, alpha = stats[hh]
            vj = jnp.concatenate([vt_ref[0, hh, b] for b in blocks], axis=1)
            acc = alpha * carry[hh][1] + jnp.dot(vj, ps[hh], preferred_element_type=F32)
            out.append((m_new, acc))
        return tuple(out)

    carry = tuple((jnp.full((1, MOBA_BLOCK), -jnp.inf, F32), jnp.zeros((V_ROWS, MOBA_BLOCK), F32))
                  for _ in range(2))
    n_far = max(((n_groups - 1) * KB_GROUP - (FAR_BLOCKS - 1)) // KB_GROUP, 0)
    ss = scores(0)
    for j in range(n_groups):
        ahead = scores(j + 1) if j + 1 < n_groups else None
        carry = softmax_pv(ss, j, carry, near=j >= n_far)
        ss = ahead
    outs = [acc[:HEAD_DIM] * (1.0 / acc[HEAD_DIM:HEAD_DIM + 1]) for _, acc in carry]
    o_ref[0] = jnp.concatenate(outs, axis=0).T.astype(BF16)


def _attention(qt, ka, vt, km, bias):
    bsz, _, seq, _ = ka.shape
    nb = seq // MOBA_BLOCK
    outs = []
    for n_groups in range(1, nb // KB_GROUP + 1):
        first = (n_groups - 1) * KB_GROUP
        keys = n_groups * KB_GROUP
        outs.append(pl.pallas_call(
            functools.partial(_attn_kernel, n_groups=n_groups),
            grid=(bsz, HEAD_PAIRS, KB_GROUP),
            in_specs=[
                pl.BlockSpec((1, 1, 1, LANES, MOBA_BLOCK), lambda b, p, r, first=first: (b, p, first + r, 0, 0)),
                pl.BlockSpec((1, 2, keys * MOBA_BLOCK, LANES), lambda b, p, r: (b, p, 0, 0)),
                pl.BlockSpec((1, 2, keys, V_ROWS, MOBA_BLOCK), lambda b, p, r: (b, p, 0, 0, 0)),
                pl.BlockSpec((1, 1, nb, LANES), lambda b, p, r: (b, p, 0, 0)),
                pl.BlockSpec((2, N_BIAS_TILES, MOBA_BLOCK, MOBA_BLOCK), lambda b, p, r: (p, 0, 0, 0)),
            ],
            out_specs=pl.BlockSpec((1, MOBA_BLOCK, LANES), lambda b, p, r: (b, r, p)),
            out_shape=jax.ShapeDtypeStruct((bsz, KB_GROUP * MOBA_BLOCK, ATTN_WIDTH), BF16),
            compiler_params=_params("parallel", "parallel", "arbitrary"),
            name=f"moba_attention_g{n_groups}",
        )(qt, ka, vt, km, bias))
    return jnp.concatenate(outs, axis=1)


def _mixout_kernel(a_ref, halo_ref, zg_ref, o_ref, x_ref, mod_ref, cw_ref, cb_ref, lg_ref, lb_ref,
                   wpa_ref, wpb_ref, wo_ref, out_ref, a_ext, a_sh, y_buf):
    t = pl.program_id(1)
    a_ext[CONV_HALO:, :] = a_ref[0].astype(F32)
    a_ext[:CONV_HALO, :] = jnp.where(t > 0, halo_ref[0].astype(F32), 0.0)
    cw = cw_ref[...]
    cb = cb_ref[...]
    first = CONV_HALO - (CONV_KERNEL - 1)
    n_sh = a_sh.shape[1]
    for s in range(1, SUBLANES):
        a_sh[s - 1] = a_ext[s:s + n_sh, :]
    for r0 in range(0, TM, CONV_ROWS):
        acc = jnp.broadcast_to(cb, (CONV_ROWS, CONV_CH))
        for j in range(CONV_KERNEL):
            q, s = divmod(first + j, SUBLANES)
            lo = r0 + SUBLANES * q
            tap = a_ext[lo:lo + CONV_ROWS, :] if s == 0 else a_sh[s - 1, lo:lo + CONV_ROWS, :]
            acc = acc + cw[j:j + 1] * tap
        y_buf[r0:r0 + CONV_ROWS, :] = acc
    y = y_buf[...]
    mu = jnp.mean(y, axis=-1, keepdims=True)
    yc = y - mu
    var = jnp.mean(yc * yc, axis=-1, keepdims=True)
    yn = yc * lax.rsqrt(var + EPS) * lg_ref[...] + lb_ref[...]
    act = (yn * _sigmoid(yn)).astype(BF16)
    ya = jnp.dot(act, wpa_ref[...], preferred_element_type=F32)
    yb = jnp.dot(o_ref[0], wpb_ref[...], preferred_element_type=F32)
    ga = zg_ref[0, :, :D_MODEL].astype(F32)
    gb = zg_ref[0, :, D_MODEL:].astype(F32)
    m = (_sigmoid(ga) * ya + _sigmoid(gb) * yb).astype(BF16)
    y = jnp.dot(m, wo_ref[...], preferred_element_type=F32)
    out_ref[0] = x_ref[0] + mod_ref[0][2:3] * y


def _mixout(a, zg, o, x, mod_l, cw, cb, lg, lb, wpa, wpb, wo):
    bsz, seq, _ = x.shape
    per = TM // CONV_HALO
    tok = lambda b, t: (b, t, 0)
    return pl.pallas_call(
        _mixout_kernel,
        grid=(bsz, seq // TM),
        in_specs=[
            pl.BlockSpec((1, TM, CONV_CH), tok),
            pl.BlockSpec((1, CONV_HALO, CONV_CH), lambda b, t: (b, jnp.maximum(t * per - 1, 0), 0)),
            pl.BlockSpec((1, TM, 2 * D_MODEL), tok),
            pl.BlockSpec((1, TM, ATTN_WIDTH), tok),
            pl.BlockSpec((1, TM, D_MODEL), tok),
            pl.BlockSpec((1, SUBLANES, D_MODEL), lambda b, t: (b, 0, 0)),
            _const_spec((CONV_HALO, CONV_CH)),
            _const_spec((1, CONV_CH)),
            _const_spec((1, CONV_CH)),
            _const_spec((1, CONV_CH)),
            _const_spec((CONV_CH, D_MODEL)),
            _const_spec((ATTN_WIDTH, D_MODEL)),
            _const_spec((D_MODEL, D_MODEL)),
        ],
        out_specs=pl.BlockSpec((1, TM, D_MODEL), tok),
        out_shape=jax.ShapeDtypeStruct((bsz, seq, D_MODEL), F32),
        scratch_shapes=[pltpu.VMEM((TM + CONV_HALO, CONV_CH), F32),
                        pltpu.VMEM((SUBLANES - 1, TM + CONV_HALO - SUBLANES, CONV_CH), F32),
                        pltpu.VMEM((TM, CONV_CH), F32)],
        compiler_params=_params("parallel", "arbitrary"),
        name="mixer_out",
    )(a, a, zg, o, x, mod_l, cw, cb, lg, lb, wpa, wpb, wo)


def _ffn_kernel(x_ref, halo_ref, mod_ref, g_ref, wup_ref, fcw_ref, fcb_ref, wdn_ref, out_ref, act_buf):
    t = pl.program_id(1)
    mod = mod_ref[0]
    x = x_ref[0]
    xe = jnp.concatenate([halo_ref[0], x], axis=0)
    h = _mod_rmsnorm(xe, g_ref[...], mod[3:4], mod[4:5])
    row = lax.broadcasted_iota(jnp.int32, (FFN_HALO + TM, 1), 0)
    h = jnp.where((row >= FFN_HALO) | (t > 0), h, 0.0).astype(BF16)

    def conv(col):
        u = jnp.dot(h, wup_ref[:, col:col + FF_CHUNK], preferred_element_type=F32)
        w = fcw_ref[:, col:col + FF_CHUNK]
        y = w[2:3] * u + w[1:2] * pltpu.roll(u, 1, 0) + w[0:1] * pltpu.roll(u, 2, 0)
        return y[FFN_HALO:] + fcb_ref[:, col:col + FF_CHUNK]

    acc = None
    group = FF_GROUP * FF_CHUNK
    for g0 in range(0, D_FF, group):
        g1 = min(g0 + group, D_FF)
        for c0 in range(g0, g1, FF_CHUNK):
            uv = conv(c0)
            ug = conv(D_FF + c0)
            act_buf[:, c0:c0 + FF_CHUNK] = (ug * _sigmoid(ug) * uv).astype(BF16)
        part = jnp.dot(act_buf[:, g0:g1], wdn_ref[g0:g1, :], preferred_element_type=F32)
        acc = part if acc is None else acc + part
    out_ref[0] = x + mod[5:6] * acc


def _ffn(x, mod_l, g, wup, fcw, fcb, wdn):
    bsz, seq, _ = x.shape
    per = TM // FFN_HALO
    return pl.pallas_call(
        _ffn_kernel,
        grid=(bsz, seq // TM),
        in_specs=[
            pl.BlockSpec((1, TM, D_MODEL), lambda b, t: (b, t, 0)),
            pl.BlockSpec((1, FFN_HALO, D_MODEL), lambda b, t: (b, jnp.maximum(t * per - 1, 0), 0)),
            pl.BlockSpec((1, SUBLANES, D_MODEL), lambda b, t: (b, 0, 0)),
            _const_spec((1, D_MODEL)),
            _const_spec((D_MODEL, 2 * D_FF)),
            _const_spec((SUBLANES, 2 * D_FF)),
            _const_spec((1, 2 * D_FF)),
            _const_spec((D_FF, D_MODEL)),
        ],
        out_specs=pl.BlockSpec((1, TM, D_MODEL), lambda b, t: (b, t, 0)),
        out_shape=jax.ShapeDtypeStruct((bsz, seq, D_MODEL), F32),
        scratch_shapes=[pltpu.VMEM((TM, D_FF), BF16)],
        compiler_params=_params("parallel", "arbitrary"),
        name="conv_ffn",
    )(x, x, mod_l, g, wup, fcw, fcb, wdn)


def kernel(x, c, ada_w, ada_b, norm1_g, w_in, conv_w, conv_b, conv_ln_g, conv_ln_b, q_norm_g, k_norm_g,
           w_pa, w_pb, w_o, norm2_g, w_up, ffn_conv_w, ffn_conv_b, w_down, rel_bias):
    mod = _modulation(c, ada_w, ada_b)
    bias = _bias_tiles(rel_bias)
    cw = jnp.pad(conv_w, ((0, 0), (0, CONV_HALO - CONV_KERNEL), (0, 0)))
    fcw = jnp.pad(ffn_conv_w, ((0, 0), (0, SUBLANES - FFN_CONV), (0, 0)))
    for l in range(DEPTH):
        a, zg, qt, ka, vt, km = _inproj(x, mod[l], norm1_g[l][None], w_in[l].astype(BF16),
                                        jnp.tile(q_norm_g[l], 2)[None], jnp.tile(k_norm_g[l], 2)[None])
        o = _attention(qt, ka, vt, km.transpose(0, 2, 1, 3), bias)
        x = _mixout(a, zg, o, x, mod[l], cw[l], conv_b[l][None], conv_ln_g[l][None], conv_ln_b[l][None],
                    w_pa[l].astype(BF16), w_pb[l].astype(BF16), w_o[l].astype(BF16))
        x = _ffn(x, mod[l], norm2_g[l][None], w_up[l].astype(BF16), fcw[l], ffn_conv_b[l][None],
                 w_down[l].astype(BF16))
    return x
```

```python
import functools
import math

import jax
import jax.numpy as jnp
from jax import lax
from jax.experimental import pallas as pl
from jax.experimental.pallas import tpu as pltpu

F32 = jnp.float32
BF16 = jnp.bfloat16

D_MODEL = 1024
DEPTH = 4
CONV_CH = 512
CONV_KERNEL = 31
HEAD_DIM = 64
N_HEADS = 8
ATTN_WIDTH = 512
MOBA_BLOCK = 256
MOBA_TOPK = 3
NUM_BUCKETS = 32
MAX_DISTANCE = 2048
D_FF = 2816
FFN_CONV = 3
EPS = 1e-6
IN_WIDTH = 2 * CONV_CH + 3 * ATTN_WIDTH + 2 * D_MODEL
Q_COL = 2 * CONV_CH
K_COL = Q_COL + ATTN_WIDTH
V_COL = K_COL + ATTN_WIDTH
GATE_COL = V_COL + ATTN_WIDTH

LANES = 128
SUBLANES = 8
HEAD_PAIRS = N_HEADS // 2
LOG2E = math.log2(math.e)
QK_SCALE = HEAD_DIM ** -0.5
NEG = -30000.0
V_ROWS = HEAD_DIM + 16
FAR_BLOCKS = -(-(MAX_DISTANCE + MOBA_BLOCK) // MOBA_BLOCK)
N_BIAS_TILES = FAR_BLOCKS + 1
CONV_HALO = 32
CONV_ROWS = 32
FFN_HALO = SUBLANES
VMEM_LIMIT = 56 * 1024 * 1024

TM = 512
FF_CHUNK = 256
FF_GROUP = 4
KB_GROUP = 4


def _sigmoid(x):
    return 1.0 / (1.0 + jnp.exp(-x))


def _params(*sem):
    return pltpu.CompilerParams(dimension_semantics=sem, vmem_limit_bytes=VMEM_LIMIT)


def _const_spec(shape):
    nd = len(shape)
    return pl.BlockSpec(shape, lambda *_: (0,) * nd, pipeline_mode=pl.Buffered(1))


def _mod_kernel(c_ref, w_ref, b_ref, o_ref):
    c = c_ref[...]
    cond = c * _sigmoid(c)
    o_ref[0] = jnp.dot(cond, w_ref[0], precision=lax.Precision.HIGHEST,
                       preferred_element_type=F32) + b_ref[0]


def _modulation(c, ada_w, ada_b):
    bsz = c.shape[0]
    c8 = jnp.pad(c, ((0, SUBLANES - bsz), (0, 0)))
    tn = 1536
    n = 6 * D_MODEL
    out = pl.pallas_call(
        _mod_kernel,
        grid=(DEPTH, n // tn),
        in_specs=[
            pl.BlockSpec((SUBLANES, D_MODEL), lambda l, j: (0, 0)),
            pl.BlockSpec((1, D_MODEL, tn), lambda l, j: (l, 0, j)),
            pl.BlockSpec((1, 1, tn), lambda l, j: (l, 0, j)),
        ],
        out_specs=pl.BlockSpec((1, SUBLANES, tn), lambda l, j: (l, 0, j)),
        out_shape=jax.ShapeDtypeStruct((DEPTH, SUBLANES, n), F32),
        compiler_params=_params("arbitrary", "arbitrary"),
        name="adaln_modulation",
    )(c8, ada_w, ada_b.reshape(DEPTH, 1, n))
    mod = out[:, :bsz].reshape(DEPTH, bsz, 6, D_MODEL)
    return jnp.pad(mod, ((0, 0), (0, 0), (0, SUBLANES - 6), (0, 0)))


def _mod_rmsnorm(x, g, shift, scale):
    ms = jnp.mean(x * x, axis=-1, keepdims=True)
    return (x * lax.rsqrt(ms + EPS) * g) * (1.0 + scale) + shift


def _inproj_kernel(x_ref, mod_ref, g_ref, w_ref, gq_ref, gk_ref, a_ref, zg_ref, qa_ref, ka_ref, vt_ref, km_s):
    t = pl.program_id(1)

    @pl.when(t == 0)
    def _():
        km_s[...] = jnp.zeros(km_s.shape, F32)

    mod = mod_ref[0]
    h = _mod_rmsnorm(x_ref[0], g_ref[...], mod[0:1], mod[1:2]).astype(BF16)
    tn = 512

    def proj(col):
        return jnp.dot(h, w_ref[:, col:col + tn], preferred_element_type=F32)

    zq, zk, zv = proj(Q_COL), proj(K_COL), proj(V_COL)

    def glu():
        a_ref[0] = (proj(0) * _sigmoid(proj(CONV_CH))).astype(BF16)

    def gate_cols(n0):
        zg_ref[0, :, n0:n0 + tn] = proj(GATE_COL + n0).astype(BF16)

    rest = [glu] + [functools.partial(gate_cols, n0) for n0 in range(0, 2 * D_MODEL, tn)]
    lane = lax.broadcasted_iota(jnp.int32, (MOBA_BLOCK, LANES), 1)
    low = lane < HEAD_DIM

    def head_norm(tv, g):
        t2 = tv * tv
        s0 = jnp.sum(jnp.where(low, t2, 0.0), axis=-1, keepdims=True)
        s1 = jnp.sum(jnp.where(low, 0.0, t2), axis=-1, keepdims=True)
        ms = jnp.where(low, s0, s1) * (1.0 / HEAD_DIM)
        return tv * lax.rsqrt(ms + EPS) * g

    blocks_per_tile = TM // MOBA_BLOCK
    nblk = km_s.shape[1]
    ones_rows = jnp.where(lax.broadcasted_iota(jnp.int32, (V_ROWS - HEAD_DIM, MOBA_BLOCK), 0) == 0, 1.0, 0.0)
    feat = lax.broadcasted_iota(jnp.int32, (LANES, MOBA_BLOCK), 0)
    blk = lax.broadcasted_iota(jnp.int32, (nblk, MOBA_BLOCK), 0)
    for sb in range(blocks_per_tile):
        r0 = sb * MOBA_BLOCK
        own = t * blocks_per_tile + sb
        onehot = jnp.where(lane - HEAD_DIM == own, 1.0, 0.0)
        past = blk < own
        for p in range(HEAD_PAIRS):
            if rest:
                rest.pop(0)()
            c0 = p * LANES
            kn = head_norm(zk[r0:r0 + MOBA_BLOCK, c0:c0 + LANES], gk_ref[...])
            km_s[p, pl.ds(own, 1), :] = jnp.mean(kn, axis=0, keepdims=True)
            qt = head_norm(zq[r0:r0 + MOBA_BLOCK, c0:c0 + LANES], gq_ref[...]).T
            vt = zv[r0:r0 + MOBA_BLOCK, c0:c0 + LANES].T
            for hh, kh in enumerate((kn, pltpu.roll(kn, HEAD_DIM, 1))):
                ka_ref[0, 2 * p + hh, r0:r0 + MOBA_BLOCK, :] = jnp.where(low, kh, onehot).astype(BF16)
                vh = vt[hh * HEAD_DIM:(hh + 1) * HEAD_DIM]
                vt_ref[0, 2 * p + hh, sb] = jnp.concatenate([vh, ones_rows], axis=0).astype(BF16)
                hmask = (feat >= hh * HEAD_DIM) & (feat < (hh + 1) * HEAD_DIM)
                gate = jnp.dot(km_s[p], jnp.where(hmask, qt, 0.0), precision=lax.Precision.HIGHEST,
                               preferred_element_type=F32)
                g = jnp.where(past, gate, -jnp.inf)
                sel = blk == own
                for _ in range(MOBA_TOPK):
                    mx = jnp.max(g, axis=0, keepdims=True)
                    idx = jnp.min(jnp.where(g == mx, blk, nblk), axis=0, keepdims=True)
                    hit = blk == idx
                    sel = sel | (hit & past)
                    g = jnp.where(hit, -jnp.inf, g)
                qa_ref[0, 2 * p + hh, sb] = jnp.concatenate(
                    [qt[hh * HEAD_DIM:(hh + 1) * HEAD_DIM] * (QK_SCALE * LOG2E), jnp.where(sel, 0.0, NEG),
                     jnp.zeros((LANES - HEAD_DIM - nblk, MOBA_BLOCK), F32)], axis=0).astype(BF16)
    for task in rest:
        task()


def _inproj(x, mod_l, g, w, gq2, gk2):
    bsz, seq, _ = x.shape
    nb = seq // MOBA_BLOCK
    bpt = TM // MOBA_BLOCK
    tok = lambda b, t: (b, t, 0)
    return pl.pallas_call(
        _inproj_kernel,
        grid=(bsz, seq // TM),
        in_specs=[
            pl.BlockSpec((1, TM, D_MODEL), tok),
            pl.BlockSpec((1, SUBLANES, D_MODEL), lambda b, t: (b, 0, 0)),
            _const_spec((1, D_MODEL)),
            _const_spec((D_MODEL, IN_WIDTH)),
            _const_spec((1, LANES)),
            _const_spec((1, LANES)),
        ],
        out_specs=[
            pl.BlockSpec((1, TM, CONV_CH), tok),
            pl.BlockSpec((1, TM, 2 * D_MODEL), tok),
            pl.BlockSpec((1, N_HEADS, bpt, LANES, MOBA_BLOCK), lambda b, t: (b, 0, t, 0, 0)),
            pl.BlockSpec((1, N_HEADS, TM, LANES), lambda b, t: (b, 0, t, 0)),
            pl.BlockSpec((1, N_HEADS, bpt, V_ROWS, MOBA_BLOCK), lambda b, t: (b, 0, t, 0, 0)),
        ],
        out_shape=[
            jax.ShapeDtypeStruct((bsz, seq, CONV_CH), BF16),
            jax.ShapeDtypeStruct((bsz, seq, 2 * D_MODEL), BF16),
            jax.ShapeDtypeStruct((bsz, N_HEADS, nb, LANES, MOBA_BLOCK), BF16),
            jax.ShapeDtypeStruct((bsz, N_HEADS, seq, LANES), BF16),
            jax.ShapeDtypeStruct((bsz, N_HEADS, nb, V_ROWS, MOBA_BLOCK), BF16),
        ],
        scratch_shapes=[pltpu.VMEM((HEAD_PAIRS, nb, LANES), F32)],
        compiler_params=_params("arbitrary", "arbitrary"),
        name="norm_inproj",
    )(x, mod_l, g, w, gq2, gk2)


def _bias_tile_kernel(rb_ref, o_ref):
    h = pl.program_id(0)
    dblk = pl.program_id(1)
    k = lax.broadcasted_iota(jnp.int32, (MOBA_BLOCK, MOBA_BLOCK), 0)
    q = lax.broadcasted_iota(jnp.int32, (MOBA_BLOCK, MOBA_BLOCK), 1)
    d = dblk * MOBA_BLOCK + q - k
    dist = jnp.maximum(d, 0)
    max_exact = NUM_BUCKETS // 2
    nf = jnp.maximum(dist, 1).astype(F32)
    large = max_exact + (jnp.log(nf / max_exact) / math.log(MAX_DISTANCE / max_exact)
                         * (NUM_BUCKETS - max_exact)).astype(jnp.int32)
    large = jnp.minimum(large, NUM_BUCKETS - 1)
    bucket = jnp.where(dist < max_exact, dist, large)
    val = jnp.zeros((MOBA_BLOCK, MOBA_BLOCK), F32)
    for b in range(NUM_BUCKETS):
        val = jnp.where(bucket == b, rb_ref[h, b], val)
    o_ref[0, 0] = jnp.where(d >= 0, val * LOG2E, NEG)


def _bias_tiles(rel_bias):
    return pl.pallas_call(
        _bias_tile_kernel,
        grid=(N_HEADS, N_BIAS_TILES),
        in_specs=[pl.BlockSpec(memory_space=pltpu.SMEM)],
        out_specs=pl.BlockSpec((1, 1, MOBA_BLOCK, MOBA_BLOCK), lambda h, d: (h, d, 0, 0)),
        out_shape=jax.ShapeDtypeStruct((N_HEADS, N_BIAS_TILES, MOBA_BLOCK, MOBA_BLOCK), F32),
        compiler_params=_params("arbitrary", "arbitrary"),
        name="t5_bias_tiles",
    )(rel_bias)


def _attn_kernel(qa_ref, ka_ref, vt_ref, bias_ref, o_ref, *, n_groups):
    i = (n_groups - 1) * KB_GROUP + pl.program_id(2)
    q_augs = [qa_ref[0, hh, 0] for hh in range(2)]

    def scores(j):
        rows = [slice((KB_GROUP * j + u) * MOBA_BLOCK, (KB_GROUP * j + u + 1) * MOBA_BLOCK) for u in range(KB_GROUP)]
        return [[jnp.dot(ka_ref[0, hh, r, :], q_augs[hh], preferred_element_type=F32) for r in rows]
                for hh in range(2)]

    def softmax_pv(ss, j, carry, near):
        blocks = [KB_GROUP * j + u for u in range(KB_GROUP)]
        ps, stats = [], []
        for hh in range(2):
            m, _ = carry[hh]
            if near:
                s = [sv + bias_ref[hh, jnp.clip(i - b, 0, FAR_BLOCKS)] for sv, b in zip(ss[hh], blocks)]
                shift_by = 0.0
            else:
                s = ss[hh]
                shift_by = bias_ref[hh, FAR_BLOCKS, 0:1, 0:1]
            mx = jnp.max(functools.reduce(jnp.maximum, s), axis=0, keepdims=True)
            m_new = jnp.maximum(m, mx + shift_by)
            shift = m_new - shift_by
            alpha = jnp.exp2(m - m_new)
            stats.append((m_new, alpha))
            ps.append(jnp.concatenate([jnp.exp2(sv - shift).astype(BF16) for sv in s], axis=0))
        out = []
        for hh in range(2):
            m_new, alpha = stats[hh]
            vj = jnp.concatenate([vt_ref[0, hh, b] for b in blocks], axis=1)
            acc = alpha * carry[hh][1] + jnp.dot(vj, ps[hh], preferred_element_type=F32)
            out.append((m_new, acc))
        return tuple(out)

    carry = tuple((jnp.full((1, MOBA_BLOCK), -jnp.inf, F32), jnp.zeros((V_ROWS, MOBA_BLOCK), F32))
                  for _ in range(2))
    n_far = max(((n_groups - 1) * KB_GROUP - (FAR_BLOCKS - 1)) // KB_GROUP, 0)
    ss = scores(0)
    for j in range(n_groups):
        ahead = scores(j + 1) if j + 1 < n_groups else None
        carry = softmax_pv(ss, j, carry, near=j >= n_far)
        ss = ahead
    outs = [acc[:HEAD_DIM] * (1.0 / acc[HEAD_DIM:HEAD_DIM + 1]) for _, acc in carry]
    o_ref[0] = jnp.concatenate(outs, axis=0).T.astype(BF16)


def _attn_instance(qa_ref, ka_ref, vt_ref, bias_ref, prev_ref, o_ref, *, n_groups):
    del prev_ref
    _attn_kernel(qa_ref, ka_ref, vt_ref, bias_ref, o_ref, n_groups=n_groups)


def _attention(qa, ka, vt, bias):
    bsz, _, seq, _ = ka.shape
    nb = seq // MOBA_BLOCK
    out = jnp.zeros((bsz, seq, ATTN_WIDTH), BF16)
    for n_groups in range(1, nb // KB_GROUP + 1):
        first = (n_groups - 1) * KB_GROUP
        keys = n_groups * KB_GROUP
        out = pl.pallas_call(
            functools.partial(_attn_instance, n_groups=n_groups),
            grid=(bsz, HEAD_PAIRS, KB_GROUP),
            in_specs=[
                pl.BlockSpec((1, 2, 1, LANES, MOBA_BLOCK), lambda b, p, r, first=first: (b, p, first + r, 0, 0)),
                pl.BlockSpec((1, 2, keys * MOBA_BLOCK, LANES), lambda b, p, r: (b, p, 0, 0)),
                pl.BlockSpec((1, 2, keys, V_ROWS, MOBA_BLOCK), lambda b, p, r: (b, p, 0, 0, 0)),
                pl.BlockSpec((2, N_BIAS_TILES, MOBA_BLOCK, MOBA_BLOCK), lambda b, p, r: (p, 0, 0, 0)),
                pl.BlockSpec(memory_space=pl.ANY),
            ],
            out_specs=pl.BlockSpec((1, MOBA_BLOCK, LANES), lambda b, p, r, first=first: (b, first + r, p)),
            out_shape=jax.ShapeDtypeStruct((bsz, seq, ATTN_WIDTH), BF16),
            input_output_aliases={4: 0},
            compiler_params=_params("parallel", "parallel", "arbitrary"),
            name=f"moba_attention_g{n_groups}",
        )(qa, ka, vt, bias, out)
    return out


def _mixout_kernel(a_ref, halo_ref, zg_ref, o_ref, x_ref, mod_ref, cw_ref, cb_ref, lg_ref, lb_ref,
                   wpa_ref, wpb_ref, wo_ref, out_ref, a_ext, a_sh, y_buf):
    t = pl.program_id(1)
    a_ext[CONV_HALO:, :] = a_ref[0].astype(F32)
    a_ext[:CONV_HALO, :] = jnp.where(t > 0, halo_ref[0].astype(F32), 0.0)
    cw = cw_ref[...]
    cb = cb_ref[...]
    first = CONV_HALO - (CONV_KERNEL - 1)
    n_sh = a_sh.shape[1]
    for s in range(1, SUBLANES):
        a_sh[s - 1] = a_ext[s:s + n_sh, :]
    for r0 in range(0, TM, CONV_ROWS):
        acc = jnp.broadcast_to(cb, (CONV_ROWS, CONV_CH))
        for j in range(CONV_KERNEL):
            q, s = divmod(first + j, SUBLANES)
            lo = r0 + SUBLANES * q
            tap = a_ext[lo:lo + CONV_ROWS, :] if s == 0 else a_sh[s - 1, lo:lo + CONV_ROWS, :]
            acc = acc + cw[j:j + 1] * tap
        y_buf[r0:r0 + CONV_ROWS, :] = acc
    y = y_buf[...]
    mu = jnp.mean(y, axis=-1, keepdims=True)
    yc = y - mu
    var = jnp.mean(yc * yc, axis=-1, keepdims=True)
    yn = yc * lax.rsqrt(var + EPS) * lg_ref[...] + lb_ref[...]
    act = (yn * _sigmoid(yn)).astype(BF16)
    ya = jnp.dot(act, wpa_ref[...], preferred_element_type=F32)
    yb = jnp.dot(o_ref[0], wpb_ref[...], preferred_element_type=F32)
    ga = zg_ref[0, :, :D_MODEL].astype(F32)
    gb = zg_ref[0, :, D_MODEL:].astype(F32)
    m = (_sigmoid(ga) * ya + _sigmoid(gb) * yb).astype(BF16)
    y = jnp.dot(m, wo_ref[...], preferred_element_type=F32)
    out_ref[0] = x_ref[0] + mod_ref[0][2:3] * y


def _mixout(a, zg, o, x, mod_l, cw, cb, lg, lb, wpa, wpb, wo):
    bsz, seq, _ = x.shape
    per = TM // CONV_HALO
    tok = lambda b, t: (b, t, 0)
    return pl.pallas_call(
        _mixout_kernel,
        grid=(bsz, seq // TM),
        in_specs=[
            pl.BlockSpec((1, TM, CONV_CH), tok),
            pl.BlockSpec((1, CONV_HALO, CONV_CH), lambda b, t: (b, jnp.maximum(t * per - 1, 0), 0)),
            pl.BlockSpec((1, TM, 2 * D_MODEL), tok),
            pl.BlockSpec((1, TM, ATTN_WIDTH), tok),
            pl.BlockSpec((1, TM, D_MODEL), tok),
            pl.BlockSpec((1, SUBLANES, D_MODEL), lambda b, t: (b, 0, 0)),
            _const_spec((CONV_HALO, CONV_CH)),
            _const_spec((1, CONV_CH)),
            _const_spec((1, CONV_CH)),
            _const_spec((1, CONV_CH)),
            _const_spec((CONV_CH, D_MODEL)),
            _const_spec((ATTN_WIDTH, D_MODEL)),
            _const_spec((D_MODEL, D_MODEL)),
        ],
        out_specs=pl.BlockSpec((1, TM, D_MODEL), tok),
        out_shape=jax.ShapeDtypeStruct((bsz, seq, D_MODEL), F32),
        scratch_shapes=[pltpu.VMEM((TM + CONV_HALO, CONV_CH), F32),
                        pltpu.VMEM((SUBLANES - 1, TM + CONV_HALO - SUBLANES, CONV_CH), F32),
                        pltpu.VMEM((TM, CONV_CH), F32)],
        compiler_params=_params("parallel", "arbitrary"),
        name="mixer_out",
    )(a, a, zg, o, x, mod_l, cw, cb, lg, lb, wpa, wpb, wo)


def _ffn_kernel(x_ref, halo_ref, mod_ref, g_ref, wup_ref, fcw_ref, fcb_ref, wdn_ref, out_ref, act_buf):
    t = pl.program_id(1)
    mod = mod_ref[0]
    x = x_ref[0]
    xe = jnp.concatenate([halo_ref[0], x], axis=0)
    h = _mod_rmsnorm(xe, g_ref[...], mod[3:4], mod[4:5])
    row = lax.broadcasted_iota(jnp.int32, (FFN_HALO + TM, 1), 0)
    h = jnp.where((row >= FFN_HALO) | (t > 0), h, 0.0).astype(BF16)

    def conv(col):
        u = jnp.dot(h, wup_ref[:, col:col + FF_CHUNK], preferred_element_type=F32)
        w = fcw_ref[:, col:col + FF_CHUNK]
        y = w[2:3] * u + w[1:2] * pltpu.roll(u, 1, 0) + w[0:1] * pltpu.roll(u, 2, 0)
        return y[FFN_HALO:] + fcb_ref[:, col:col + FF_CHUNK]

    def down(g0, g1):
        return jnp.dot(act_buf[:, g0:g1], wdn_ref[g0:g1, :], preferred_element_type=F32)

    acc = None
    pending = None
    group = FF_GROUP * FF_CHUNK
    for g0 in range(0, D_FF, group):
        g1 = min(g0 + group, D_FF)
        for c0 in range(g0, g1, FF_CHUNK):
            uv = conv(c0)
            ug = conv(D_FF + c0)
            act_buf[:, c0:c0 + FF_CHUNK] = (ug * _sigmoid(ug) * uv).astype(BF16)
            if pending is not None:
                part = down(*pending)
                acc = part if acc is None else acc + part
                pending = None
        pending = (g0, g1)
    part = down(*pending)
    acc = part if acc is None else acc + part
    out_ref[0] = x + mod[5:6] * acc


def _ffn(x, mod_l, g, wup, fcw, fcb, wdn):
    bsz, seq, _ = x.shape
    per = TM // FFN_HALO
    return pl.pallas_call(
        _ffn_kernel,
        grid=(bsz, seq // TM),
        in_specs=[
            pl.BlockSpec((1, TM, D_MODEL), lambda b, t: (b, t, 0)),
            pl.BlockSpec((1, FFN_HALO, D_MODEL), lambda b, t: (b, jnp.maximum(t * per - 1, 0), 0)),
            pl.BlockSpec((1, SUBLANES, D_MODEL), lambda b, t: (b, 0, 0)),
            _const_spec((1, D_MODEL)),
            _const_spec((D_MODEL, 2 * D_FF)),
            _const_spec((SUBLANES, 2 * D_FF)),
            _const_spec((1, 2 * D_FF)),
            _const_spec((D_FF, D_MODEL)),
        ],
        out_specs=pl.BlockSpec((1, TM, D_MODEL), lambda b, t: (b, t, 0)),
        out_shape=jax.ShapeDtypeStruct((bsz, seq, D_MODEL), F32),
        scratch_shapes=[pltpu.VMEM((TM, D_FF), BF16)],
        compiler_params=_params("parallel", "arbitrary"),
        name="conv_ffn",
    )(x, x, mod_l, g, wup, fcw, fcb, wdn)


def kernel(x, c, ada_w, ada_b, norm1_g, w_in, conv_w, conv_b, conv_ln_g, conv_ln_b, q_norm_g, k_norm_g,
           w_pa, w_pb, w_o, norm2_g, w_up, ffn_conv_w, ffn_conv_b, w_down, rel_bias):
    mod = _modulation(c, ada_w, ada_b)
    bias = _bias_tiles(rel_bias)
    cw = jnp.pad(conv_w, ((0, 0), (0, CONV_HALO - CONV_KERNEL), (0, 0)))
    fcw = jnp.pad(ffn_conv_w, ((0, 0), (0, SUBLANES - FFN_CONV), (0, 0)))
    for l in range(DEPTH):
        a, zg, qa, ka, vt = _inproj(x, mod[l], norm1_g[l][None], w_in[l].astype(BF16),
                                    jnp.tile(q_norm_g[l], 2)[None], jnp.tile(k_norm_g[l], 2)[None])
        o = _attention(qa, ka, vt, bias)
        x = _mixout(a, zg, o, x, mod[l], cw[l], conv_b[l][None], conv_ln_g[l][None], conv_ln_b[l][None],
                    w_pa[l].astype(BF16), w_pb[l].astype(BF16), w_o[l].astype(BF16))
        x = _ffn(x, mod[l], norm2_g[l][None], w_up[l].astype(BF16), fcw[l], ffn_conv_b[l][None],
                 w_down[l].astype(BF16))
    return x
```

```python
import functools
import math

import jax
import jax.numpy as jnp
from jax import lax
from jax.experimental import pallas as pl
from jax.experimental.pallas import tpu as pltpu

F32 = jnp.float32
BF16 = jnp.bfloat16

D_MODEL = 1024
DEPTH = 4
CONV_CH = 512
CONV_KERNEL = 31
HEAD_DIM = 64
N_HEADS = 8
ATTN_WIDTH = 512
MOBA_BLOCK = 256
MOBA_TOPK = 3
NUM_BUCKETS = 32
MAX_DISTANCE = 2048
D_FF = 2816
FFN_CONV = 3
EPS = 1e-6
IN_WIDTH = 2 * CONV_CH + 3 * ATTN_WIDTH + 2 * D_MODEL
Q_COL = 2 * CONV_CH
K_COL = Q_COL + ATTN_WIDTH
V_COL = K_COL + ATTN_WIDTH
GATE_COL = V_COL + ATTN_WIDTH

LANES = 128
SUBLANES = 8
HEAD_PAIRS = N_HEADS // 2
LOG2E = math.log2(math.e)
QK_SCALE = HEAD_DIM ** -0.5
NEG = -30000.0
V_ROWS = HEAD_DIM + 16
FAR_BLOCKS = -(-(MAX_DISTANCE + MOBA_BLOCK) // MOBA_BLOCK)
N_BIAS_TILES = FAR_BLOCKS + 1
CONV_HALO = 32
CONV_ROWS = 32
FFN_HALO = SUBLANES
VMEM_LIMIT = 56 * 1024 * 1024

TM = 512
FF_CHUNK = 256
FF_GROUP = 4
KB_GROUP = 4
Q_CLASS = 4


def _sigmoid(x):
    return 1.0 / (1.0 + jnp.exp(-x))


def _params(*sem):
    return pltpu.CompilerParams(dimension_semantics=sem, vmem_limit_bytes=VMEM_LIMIT)


def _layer_spec(l, shape):
    nd = len(shape)
    return pl.BlockSpec((None,) + shape, lambda *_: (l,) + (0,) * nd, pipeline_mode=pl.Buffered(1))


def _mod_spec(l):
    return pl.BlockSpec((None, 1, SUBLANES, D_MODEL), lambda b, t: (l, b, 0, 0))


def _mod_kernel(c_ref, w_ref, b_ref, o_ref):
    c = c_ref[...]
    cond = c * _sigmoid(c)
    o_ref[0] = jnp.dot(cond, w_ref[0], precision=lax.Precision.HIGHEST,
                       preferred_element_type=F32) + b_ref[0]


def _modulation(c, ada_w, ada_b):
    bsz = c.shape[0]
    c8 = jnp.pad(c, ((0, SUBLANES - bsz), (0, 0)))
    tn = 1536
    n = 6 * D_MODEL
    out = pl.pallas_call(
        _mod_kernel,
        grid=(DEPTH, n // tn),
        in_specs=[
            pl.BlockSpec((SUBLANES, D_MODEL), lambda l, j: (0, 0)),
            pl.BlockSpec((1, D_MODEL, tn), lambda l, j: (l, 0, j)),
            pl.BlockSpec((1, 1, tn), lambda l, j: (l, 0, j)),
        ],
        out_specs=pl.BlockSpec((1, SUBLANES, tn), lambda l, j: (l, 0, j)),
        out_shape=jax.ShapeDtypeStruct((DEPTH, SUBLANES, n), F32),
        compiler_params=_params("arbitrary", "arbitrary"),
        name="adaln_modulation",
    )(c8, ada_w, ada_b.reshape(DEPTH, 1, n))
    mod = out[:, :bsz].reshape(DEPTH, bsz, 6, D_MODEL)
    return jnp.pad(mod, ((0, 0), (0, 0), (0, SUBLANES - 6), (0, 0)))


def _mod_rmsnorm(x, g, shift, scale):
    ms = jnp.mean(x * x, axis=-1, keepdims=True)
    return (x * lax.rsqrt(ms + EPS) * g) * (1.0 + scale) + shift


def _inproj_kernel(x_ref, mod_ref, g_ref, w_ref, gq_ref, gk_ref, a_ref, zg_ref, qa_ref, ka_ref, vt_ref, km_s):
    t = pl.program_id(1)

    @pl.when(t == 0)
    def _():
        km_s[...] = jnp.zeros(km_s.shape, F32)

    mod = mod_ref[0]
    h = _mod_rmsnorm(x_ref[0], g_ref[...], mod[0:1], mod[1:2]).astype(BF16)
    tn = 512

    def proj(col):
        return jnp.dot(h, w_ref[:, col:col + tn], preferred_element_type=F32)

    zq, zk, zv = proj(Q_COL), proj(K_COL), proj(V_COL)

    def glu():
        a_ref[0] = (proj(0) * _sigmoid(proj(CONV_CH))).astype(BF16)

    def gate_cols(n0):
        zg_ref[0, :, n0:n0 + tn] = proj(GATE_COL + n0).astype(BF16)

    rest = [glu] + [functools.partial(gate_cols, n0) for n0 in range(0, 2 * D_MODEL, tn)]
    lane = lax.broadcasted_iota(jnp.int32, (MOBA_BLOCK, LANES), 1)
    low = lane < HEAD_DIM

    def head_norm(tv, g):
        t2 = tv * tv
        s0 = jnp.sum(jnp.where(low, t2, 0.0), axis=-1, keepdims=True)
        s1 = jnp.sum(jnp.where(low, 0.0, t2), axis=-1, keepdims=True)
        ms = jnp.where(low, s0, s1) * (1.0 / HEAD_DIM)
        return tv * lax.rsqrt(ms + EPS) * g

    blocks_per_tile = TM // MOBA_BLOCK
    nblk = km_s.shape[1]
    ones_rows = jnp.where(lax.broadcasted_iota(jnp.int32, (V_ROWS - HEAD_DIM, MOBA_BLOCK), 0) == 0, 1.0, 0.0)
    feat = lax.broadcasted_iota(jnp.int32, (LANES, MOBA_BLOCK), 0)
    blk = lax.broadcasted_iota(jnp.int32, (nblk, MOBA_BLOCK), 0)
    for sb in range(blocks_per_tile):
        r0 = sb * MOBA_BLOCK
        own = t * blocks_per_tile + sb
        onehot = jnp.where(lane - HEAD_DIM == own, 1.0, 0.0)
        past = blk < own
        for p in range(HEAD_PAIRS):
            if rest:
                rest.pop(0)()
            c0 = p * LANES
            kn = head_norm(zk[r0:r0 + MOBA_BLOCK, c0:c0 + LANES], gk_ref[...])
            km_s[p, pl.ds(own, 1), :] = jnp.mean(kn, axis=0, keepdims=True)
            qt = head_norm(zq[r0:r0 + MOBA_BLOCK, c0:c0 + LANES], gq_ref[...]).T
            vt = zv[r0:r0 + MOBA_BLOCK, c0:c0 + LANES].T
            for hh, kh in enumerate((kn, pltpu.roll(kn, HEAD_DIM, 1))):
                ka_ref[0, 2 * p + hh, r0:r0 + MOBA_BLOCK, :] = jnp.where(low, kh, onehot).astype(BF16)
                vh = vt[hh * HEAD_DIM:(hh + 1) * HEAD_DIM]
                vt_ref[0, 2 * p + hh, sb] = jnp.concatenate([vh, ones_rows], axis=0).astype(BF16)
                hmask = (feat >= hh * HEAD_DIM) & (feat < (hh + 1) * HEAD_DIM)
                gate = jnp.dot(km_s[p], jnp.where(hmask, qt, 0.0), precision=lax.Precision.HIGHEST,
                               preferred_element_type=F32)
                g = jnp.where(past, gate, -jnp.inf)
                sel = blk == own
                for _ in range(MOBA_TOPK):
                    mx = jnp.max(g, axis=0, keepdims=True)
                    idx = jnp.min(jnp.where(g == mx, blk, nblk), axis=0, keepdims=True)
                    hit = blk == idx
                    sel = sel | (hit & past)
                    g = jnp.where(hit, -jnp.inf, g)
                qa_ref[0, 2 * p + hh, sb] = jnp.concatenate(
                    [qt[hh * HEAD_DIM:(hh + 1) * HEAD_DIM] * (QK_SCALE * LOG2E), jnp.where(sel, 0.0, NEG),
                     jnp.zeros((LANES - HEAD_DIM - nblk, MOBA_BLOCK), F32)], axis=0).astype(BF16)
    for task in rest:
        task()


def _inproj(l, x, mod, g, w, gq2, gk2):
    bsz, seq, _ = x.shape
    nb = seq // MOBA_BLOCK
    bpt = TM // MOBA_BLOCK
    tok = lambda b, t: (b, t, 0)
    return pl.pallas_call(
        _inproj_kernel,
        grid=(bsz, seq // TM),
        in_specs=[
            pl.BlockSpec((1, TM, D_MODEL), tok),
            _mod_spec(l),
            _layer_spec(l, (1, D_MODEL)),
            _layer_spec(l, (D_MODEL, IN_WIDTH)),
            _layer_spec(l, (1, LANES)),
            _layer_spec(l, (1, LANES)),
        ],
        out_specs=[
            pl.BlockSpec((1, TM, CONV_CH), tok),
            pl.BlockSpec((1, TM, 2 * D_MODEL), tok),
            pl.BlockSpec((1, N_HEADS, bpt, LANES, MOBA_BLOCK), lambda b, t: (b, 0, t, 0, 0)),
            pl.BlockSpec((1, N_HEADS, TM, LANES), lambda b, t: (b, 0, t, 0)),
            pl.BlockSpec((1, N_HEADS, bpt, V_ROWS, MOBA_BLOCK), lambda b, t: (b, 0, t, 0, 0)),
        ],
        out_shape=[
            jax.ShapeDtypeStruct((bsz, seq, CONV_CH), BF16),
            jax.ShapeDtypeStruct((bsz, seq, 2 * D_MODEL), BF16),
            jax.ShapeDtypeStruct((bsz, N_HEADS, nb, LANES, MOBA_BLOCK), BF16),
            jax.ShapeDtypeStruct((bsz, N_HEADS, seq, LANES), BF16),
            jax.ShapeDtypeStruct((bsz, N_HEADS, nb, V_ROWS, MOBA_BLOCK), BF16),
        ],
        scratch_shapes=[pltpu.VMEM((HEAD_PAIRS, nb, LANES), F32)],
        compiler_params=_params("arbitrary", "arbitrary"),
        name="norm_inproj",
    )(x, mod, g, w, gq2, gk2)


BIAS_STRIP = 32


def _bias_tile_kernel(rb_ref, o_ref):
    dblk = pl.program_id(0)
    max_exact = NUM_BUCKETS // 2
    for k0 in range(0, MOBA_BLOCK, BIAS_STRIP):
        k = k0 + lax.broadcasted_iota(jnp.int32, (BIAS_STRIP, MOBA_BLOCK), 0)
        q = lax.broadcasted_iota(jnp.int32, (BIAS_STRIP, MOBA_BLOCK), 1)
        d = dblk * MOBA_BLOCK + q - k
        dist = jnp.maximum(d, 0)
        nf = jnp.maximum(dist, 1).astype(F32)
        large = max_exact + (jnp.log(nf / max_exact) / math.log(MAX_DISTANCE / max_exact)
                             * (NUM_BUCKETS - max_exact)).astype(jnp.int32)
        large = jnp.minimum(large, NUM_BUCKETS - 1)
        bucket = jnp.where(dist < max_exact, dist, large)
        for h in range(N_HEADS):
            val = jnp.zeros((BIAS_STRIP, MOBA_BLOCK), F32)
            for b in range(NUM_BUCKETS):
                val = jnp.where(bucket == b, rb_ref[h, b], val)
            o_ref[h, 0, k0:k0 + BIAS_STRIP, :] = jnp.where(d >= 0, val * LOG2E, NEG)


def _bias_tiles(rel_bias):
    return pl.pallas_call(
        _bias_tile_kernel,
        grid=(N_BIAS_TILES,),
        in_specs=[pl.BlockSpec(memory_space=pltpu.SMEM)],
        out_specs=pl.BlockSpec((N_HEADS, 1, MOBA_BLOCK, MOBA_BLOCK), lambda d: (0, d, 0, 0)),
        out_shape=jax.ShapeDtypeStruct((N_HEADS, N_BIAS_TILES, MOBA_BLOCK, MOBA_BLOCK), F32),
        compiler_params=_params("arbitrary"),
        name="t5_bias_tiles",
    )(rel_bias)


def _key_groups(first_q):
    visible = first_q + Q_CLASS
    n_far = max(first_q - (FAR_BLOCKS - 1), 0)
    groups = [(b, min(KB_GROUP, n_far - b), False) for b in range(0, n_far, KB_GROUP)]
    groups += [(b, min(KB_GROUP, visible - b), True) for b in range(n_far, visible, KB_GROUP)]
    return groups


def _attn_kernel(qa_ref, ka_ref, vt_ref, bias_ref, o_ref, *, first_q):
    i = first_q + pl.program_id(2)
    q_augs = [qa_ref[0, hh, 0] for hh in range(2)]

    def scores(group):
        b0, n, _ = group
        rows = [slice((b0 + u) * MOBA_BLOCK, (b0 + u + 1) * MOBA_BLOCK) for u in range(n)]
        return [[jnp.dot(ka_ref[0, hh, r, :], q_augs[hh], preferred_element_type=F32) for r in rows]
                for hh in range(2)]

    def softmax_pv(ss, group, carry):
        b0, n, near = group
        blocks = [b0 + u for u in range(n)]
        ps, stats = [], []
        for hh in range(2):
            m, _ = carry[hh]
            if near:
                s = [sv + bias_ref[hh, jnp.clip(i - b, 0, FAR_BLOCKS)] for sv, b in zip(ss[hh], blocks)]
                shift_by = 0.0
            else:
                s = ss[hh]
                shift_by = bias_ref[hh, FAR_BLOCKS, 0:1, 0:1]
            mx = jnp.max(functools.reduce(jnp.maximum, s), axis=0, keepdims=True)
            m_new = jnp.maximum(m, mx + shift_by)
            shift = m_new - shift_by
            alpha = jnp.exp2(m - m_new)
            stats.append((m_new, alpha))
            ps.append(jnp.concatenate([jnp.exp2(sv - shift).astype(BF16) for sv in s], axis=0))
        out = []
        for hh in range(2):
            m_new, alpha = stats[hh]
            vj = jnp.concatenate([vt_ref[0, hh, b] for b in blocks], axis=1)
            acc = alpha * carry[hh][1] + jnp.dot(vj, ps[hh], preferred_element_type=F32)
            out.append((m_new, acc))
        return tuple(out)

    carry = tuple((jnp.full((1, MOBA_BLOCK), -jnp.inf, F32), jnp.zeros((V_ROWS, MOBA_BLOCK), F32))
                  for _ in range(2))
    groups = _key_groups(first_q)
    ahead = scores(groups[0])
    for g, group in enumerate(groups):
        ss = ahead
        ahead = scores(groups[g + 1]) if g + 1 < len(groups) else None
        carry = softmax_pv(ss, group, carry)
    outs = [acc[:HEAD_DIM] * (1.0 / acc[HEAD_DIM:HEAD_DIM + 1]) for _, acc in carry]
    o_ref[0] = jnp.concatenate(outs, axis=0).T.astype(BF16)


def _attn_instance(qa_ref, ka_ref, vt_ref, bias_ref, prev_ref, o_ref, *, first_q):
    del prev_ref
    _attn_kernel(qa_ref, ka_ref, vt_ref, bias_ref, o_ref, first_q=first_q)


def _attention(qa, ka, vt, bias):
    bsz, _, seq, _ = ka.shape
    nb = seq // MOBA_BLOCK
    out = jnp.zeros((bsz, seq, ATTN_WIDTH), BF16)
    for first in range(0, nb, Q_CLASS):
        keys = first + Q_CLASS
        out = pl.pallas_call(
            functools.partial(_attn_instance, first_q=first),
            grid=(bsz, HEAD_PAIRS, Q_CLASS),
            in_specs=[
                pl.BlockSpec((1, 2, 1, LANES, MOBA_BLOCK), lambda b, p, r, first=first: (b, p, first + r, 0, 0)),
                pl.BlockSpec((1, 2, keys * MOBA_BLOCK, LANES), lambda b, p, r: (b, p, 0, 0)),
                pl.BlockSpec((1, 2, keys, V_ROWS, MOBA_BLOCK), lambda b, p, r: (b, p, 0, 0, 0)),
                pl.BlockSpec((2, N_BIAS_TILES, MOBA_BLOCK, MOBA_BLOCK), lambda b, p, r: (p, 0, 0, 0)),
                pl.BlockSpec(memory_space=pl.ANY),
            ],
            out_specs=pl.BlockSpec((1, MOBA_BLOCK, LANES), lambda b, p, r, first=first: (b, first + r, p)),
            out_shape=jax.ShapeDtypeStruct((bsz, seq, ATTN_WIDTH), BF16),
            input_output_aliases={4: 0},
            compiler_params=_params("parallel", "parallel", "arbitrary"),
            name=f"moba_attention_q{first}",
        )(qa, ka, vt, bias, out)
    return out


def _mixout_kernel(a_ref, halo_ref, zg_ref, o_ref, x_ref, mod_ref, cw_ref, cb_ref, lg_ref, lb_ref,
                   wpa_ref, wpb_ref, wo_ref, out_ref, a_ext, a_sh, y_buf):
    t = pl.program_id(1)
    a_ext[CONV_HALO:, :] = a_ref[0].astype(F32)
    a_ext[:CONV_HALO, :] = jnp.where(t > 0, halo_ref[0].astype(F32), 0.0)
    cw = cw_ref[...]
    cb = cb_ref[...]
    first = CONV_HALO - (CONV_KERNEL - 1)
    n_sh = a_sh.shape[1]
    for s in range(1, SUBLANES):
        a_sh[s - 1] = a_ext[s:s + n_sh, :]
    for r0 in range(0, TM, CONV_ROWS):
        acc = jnp.broadcast_to(cb, (CONV_ROWS, CONV_CH))
        for j in range(CONV_KERNEL):
            q, s = divmod(first + j, SUBLANES)
            lo = r0 + SUBLANES * q
            tap = a_ext[lo:lo + CONV_ROWS, :] if s == 0 else a_sh[s - 1, lo:lo + CONV_ROWS, :]
            acc = acc + cw[j:j + 1] * tap
        y_buf[r0:r0 + CONV_ROWS, :] = acc
    y = y_buf[...]
    mu = jnp.mean(y, axis=-1, keepdims=True)
    yc = y - mu
    var = jnp.mean(yc * yc, axis=-1, keepdims=True)
    yn = yc * lax.rsqrt(var + EPS) * lg_ref[...] + lb_ref[...]
    act = (yn * _sigmoid(yn)).astype(BF16)
    ya = jnp.dot(act, wpa_ref[...], preferred_element_type=F32)
    yb = jnp.dot(o_ref[0], wpb_ref[...], preferred_element_type=F32)
    ga = zg_ref[0, :, :D_MODEL].astype(F32)
    gb = zg_ref[0, :, D_MODEL:].astype(F32)
    m = (_sigmoid(ga) * ya + _sigmoid(gb) * yb).astype(BF16)
    y = jnp.dot(m, wo_ref[...], preferred_element_type=F32)
    out_ref[0] = x_ref[0] + mod_ref[0][2:3] * y


def _mixout(l, a, zg, o, x, mod, cw, cb, lg, lb, wpa, wpb, wo):
    bsz, seq, _ = x.shape
    per = TM // CONV_HALO
    tok = lambda b, t: (b, t, 0)
    return pl.pallas_call(
        _mixout_kernel,
        grid=(bsz, seq // TM),
        in_specs=[
            pl.BlockSpec((1, TM, CONV_CH), tok),
            pl.BlockSpec((1, CONV_HALO, CONV_CH), lambda b, t: (b, jnp.maximum(t * per - 1, 0), 0)),
            pl.BlockSpec((1, TM, 2 * D_MODEL), tok),
            pl.BlockSpec((1, TM, ATTN_WIDTH), tok),
            pl.BlockSpec((1, TM, D_MODEL), tok),
            _mod_spec(l),
            _layer_spec(l, (CONV_HALO, CONV_CH)),
            _layer_spec(l, (1, CONV_CH)),
            _layer_spec(l, (1, CONV_CH)),
            _layer_spec(l, (1, CONV_CH)),
            _layer_spec(l, (CONV_CH, D_MODEL)),
            _layer_spec(l, (ATTN_WIDTH, D_MODEL)),
            _layer_spec(l, (D_MODEL, D_MODEL)),
        ],
        out_specs=pl.BlockSpec((1, TM, D_MODEL), tok),
        out_shape=jax.ShapeDtypeStruct((bsz, seq, D_MODEL), F32),
        scratch_shapes=[pltpu.VMEM((TM + CONV_HALO, CONV_CH), F32),
                        pltpu.VMEM((SUBLANES - 1, TM + CONV_HALO - SUBLANES, CONV_CH), F32),
                        pltpu.VMEM((TM, CONV_CH), F32)],
        compiler_params=_params("parallel", "arbitrary"),
        name="mixer_out",
    )(a, a, zg, o, x, mod, cw, cb, lg, lb, wpa, wpb, wo)


def _ffn_kernel(x_ref, halo_ref, mod_ref, g_ref, wup_ref, fcw_ref, fcb_ref, wdn_ref, out_ref, act_buf):
    t = pl.program_id(1)
    mod = mod_ref[0]
    x = x_ref[0]
    xe = jnp.concatenate([halo_ref[0], x], axis=0)
    h = _mod_rmsnorm(xe, g_ref[...], mod[3:4], mod[4:5])
    row = lax.broadcasted_iota(jnp.int32, (FFN_HALO + TM, 1), 0)
    h = jnp.where((row >= FFN_HALO) | (t > 0), h, 0.0).astype(BF16)

    def conv(col):
        u = jnp.dot(h, wup_ref[:, col:col + FF_CHUNK], preferred_element_type=F32)
        w = fcw_ref[:, col:col + FF_CHUNK]
        y = w[2:3] * u + w[1:2] * pltpu.roll(u, 1, 0) + w[0:1] * pltpu.roll(u, 2, 0)
        return y[FFN_HALO:] + fcb_ref[:, col:col + FF_CHUNK]

    def down(g0, g1):
        return jnp.dot(act_buf[:, g0:g1], wdn_ref[g0:g1, :], preferred_element_type=F32)

    acc = None
    pending = None
    group = FF_GROUP * FF_CHUNK
    for g0 in range(0, D_FF, group):
        g1 = min(g0 + group, D_FF)
        for c0 in range(g0, g1, FF_CHUNK):
            uv = conv(c0)
            ug = conv(D_FF + c0)
            act_buf[:, c0:c0 + FF_CHUNK] = (ug * _sigmoid(ug) * uv).astype(BF16)
            if pending is not None:
                part = down(*pending)
                acc = part if acc is None else acc + part
                pending = None
        pending = (g0, g1)
    part = down(*pending)
    acc = part if acc is None else acc + part
    out_ref[0] = x + mod[5:6] * acc


def _ffn(l, x, mod, g, wup, fcw, fcb, wdn):
    bsz, seq, _ = x.shape
    per = TM // FFN_HALO
    return pl.pallas_call(
        _ffn_kernel,
        grid=(bsz, seq // TM),
        in_specs=[
            pl.BlockSpec((1, TM, D_MODEL), lambda b, t: (b, t, 0)),
            pl.BlockSpec((1, FFN_HALO, D_MODEL), lambda b, t: (b, jnp.maximum(t * per - 1, 0), 0)),
            _mod_spec(l),
            _layer_spec(l, (1, D_MODEL)),
            _layer_spec(l, (D_MODEL, 2 * D_FF)),
            _layer_spec(l, (SUBLANES, 2 * D_FF)),
            _layer_spec(l, (1, 2 * D_FF)),
            _layer_spec(l, (D_FF, D_MODEL)),
        ],
        out_specs=pl.BlockSpec((1, TM, D_MODEL), lambda b, t: (b, t, 0)),
        out_shape=jax.ShapeDtypeStruct((bsz, seq, D_MODEL), F32),
        scratch_shapes=[pltpu.VMEM((TM, D_FF), BF16)],
        compiler_params=_params("parallel", "arbitrary"),
        name="conv_ffn",
    )(x, x, mod, g, wup, fcw, fcb, wdn)


def kernel(x, c, ada_w, ada_b, norm1_g, w_in, conv_w, conv_b, conv_ln_g, conv_ln_b, q_norm_g, k_norm_g,
           w_pa, w_pb, w_o, norm2_g, w_up, ffn_conv_w, ffn_conv_b, w_down, rel_bias):
    mod = _modulation(c, ada_w, ada_b)
    bias = _bias_tiles(rel_bias)
    row = lambda p: p[:, None, :]
    cw = jnp.pad(conv_w, ((0, 0), (0, CONV_HALO - CONV_KERNEL), (0, 0)))
    fcw = jnp.pad(ffn_conv_w, ((0, 0), (0, SUBLANES - FFN_CONV), (0, 0)))
    gq2, gk2 = row(jnp.tile(q_norm_g, (1, 2))), row(jnp.tile(k_norm_g, (1, 2)))
    w_in, w_pa, w_pb, w_o, w_up, w_down = (w.astype(BF16) for w in (w_in, w_pa, w_pb, w_o, w_up, w_down))
    for l in range(DEPTH):
        a, zg, qa, ka, vt = _inproj(l, x, mod, row(norm1_g), w_in, gq2, gk2)
        o = _attention(qa, ka, vt, bias)
        x = _mixout(l, a, zg, o, x, mod, cw, row(conv_b), row(conv_ln_g), row(conv_ln_b), w_pa, w_pb, w_o)
        x = _ffn(l, x, mod, row(norm2_g), w_up, fcw, row(ffn_conv_b), w_down)
    return x
```

```python
import functools
import math

import jax
import jax.numpy as jnp
from jax import lax
from jax.experimental import pallas as pl
from jax.experimental.pallas import tpu as pltpu

F32 = jnp.float32
BF16 = jnp.bfloat16

D_MODEL = 1024
DEPTH = 4
CONV_CH = 512
CONV_KERNEL = 31
HEAD_DIM = 64
N_HEADS = 8
ATTN_WIDTH = 512
MOBA_BLOCK = 256
MOBA_TOPK = 3
NUM_BUCKETS = 32
MAX_DISTANCE = 2048
D_FF = 2816
FFN_CONV = 3
EPS = 1e-6
IN_WIDTH = 2 * CONV_CH + 3 * ATTN_WIDTH + 2 * D_MODEL
Q_COL = 2 * CONV_CH
K_COL = Q_COL + ATTN_WIDTH
V_COL = K_COL + ATTN_WIDTH
GATE_COL = V_COL + ATTN_WIDTH

LANES = 128
SUBLANES = 8
HEAD_PAIRS = N_HEADS // 2
LOG2E = math.log2(math.e)
QK_SCALE = HEAD_DIM ** -0.5
NEG = -30000.0
V_ROWS = HEAD_DIM + 16
FAR_BLOCKS = -(-(MAX_DISTANCE + MOBA_BLOCK) // MOBA_BLOCK)
N_BIAS_TILES = FAR_BLOCKS + 1
CONV_HALO = 32
CONV_ROWS = 32
FFN_HALO = SUBLANES
VMEM_LIMIT = 56 * 1024 * 1024

TM = 512
FF_CHUNK = 256
FF_GROUPS = (4, 4, 2, 1)
KB_GROUP = 4
Q_CLASS = 4
Q_STEP = 4


def _sigmoid(x):
    return 1.0 / (1.0 + jnp.exp(-x))


def _params(*sem):
    return pltpu.CompilerParams(dimension_semantics=sem, vmem_limit_bytes=VMEM_LIMIT)


def _layer_spec(l, shape):
    nd = len(shape)
    return pl.BlockSpec((None,) + shape, lambda *_: (l,) + (0,) * nd, pipeline_mode=pl.Buffered(1))


def _mod_spec(l):
    return pl.BlockSpec((None, 1, SUBLANES, D_MODEL), lambda b, t: (l, b, 0, 0))


def _mod_kernel(c_ref, w_ref, b_ref, o_ref):
    c = c_ref[...]
    cond = c * _sigmoid(c)
    o_ref[0] = jnp.dot(cond, w_ref[0], precision=lax.Precision.HIGHEST,
                       preferred_element_type=F32) + b_ref[0]


def _modulation(c, ada_w, ada_b):
    bsz = c.shape[0]
    c8 = jnp.pad(c, ((0, SUBLANES - bsz), (0, 0)))
    tn = 1536
    n = 6 * D_MODEL
    out = pl.pallas_call(
        _mod_kernel,
        grid=(DEPTH, n // tn),
        in_specs=[
            pl.BlockSpec((SUBLANES, D_MODEL), lambda l, j: (0, 0)),
            pl.BlockSpec((1, D_MODEL, tn), lambda l, j: (l, 0, j)),
            pl.BlockSpec((1, 1, tn), lambda l, j: (l, 0, j)),
        ],
        out_specs=pl.BlockSpec((1, SUBLANES, tn), lambda l, j: (l, 0, j)),
        out_shape=jax.ShapeDtypeStruct((DEPTH, SUBLANES, n), F32),
        compiler_params=_params("arbitrary", "arbitrary"),
        name="adaln_modulation",
    )(c8, ada_w, ada_b.reshape(DEPTH, 1, n))
    mod = out[:, :bsz].reshape(DEPTH, bsz, 6, D_MODEL)
    return jnp.pad(mod, ((0, 0), (0, 0), (0, SUBLANES - 6), (0, 0)))


def _mod_rmsnorm(x, g, shift, scale):
    ms = jnp.mean(x * x, axis=-1, keepdims=True)
    return (x * lax.rsqrt(ms + EPS) * g) * (1.0 + scale) + shift


def _inproj_kernel(x_ref, mod_ref, g_ref, w_ref, gq_ref, gk_ref, a_ref, zg_ref, qa_ref, ka_ref, vt_ref, km_s):
    t = pl.program_id(1)

    @pl.when(t == 0)
    def _():
        km_s[...] = jnp.zeros(km_s.shape, F32)

    mod = mod_ref[0]
    h = _mod_rmsnorm(x_ref[0], g_ref[...], mod[0:1], mod[1:2]).astype(BF16)
    tn = 512

    def proj(col):
        return jnp.dot(h, w_ref[:, col:col + tn], preferred_element_type=F32)

    zq, zk, zv = proj(Q_COL), proj(K_COL), proj(V_COL)

    def glu():
        a_ref[0] = (proj(0) * _sigmoid(proj(CONV_CH))).astype(BF16)

    def gate_cols(n0):
        zg_ref[0, :, n0:n0 + tn] = proj(GATE_COL + n0).astype(BF16)

    rest = [glu] + [functools.partial(gate_cols, n0) for n0 in range(0, 2 * D_MODEL, tn)]
    lane = lax.broadcasted_iota(jnp.int32, (MOBA_BLOCK, LANES), 1)
    low = lane < HEAD_DIM

    def head_norm(tv, g):
        t2 = tv * tv
        s0 = jnp.sum(jnp.where(low, t2, 0.0), axis=-1, keepdims=True)
        s1 = jnp.sum(jnp.where(low, 0.0, t2), axis=-1, keepdims=True)
        ms = jnp.where(low, s0, s1) * (1.0 / HEAD_DIM)
        return tv * lax.rsqrt(ms + EPS) * g

    blocks_per_tile = TM // MOBA_BLOCK
    nblk = km_s.shape[1]
    ones_rows = jnp.where(lax.broadcasted_iota(jnp.int32, (V_ROWS - HEAD_DIM, MOBA_BLOCK), 0) == 0, 1.0, 0.0)
    feat = lax.broadcasted_iota(jnp.int32, (LANES, MOBA_BLOCK), 0)
    blk = lax.broadcasted_iota(jnp.int32, (nblk, MOBA_BLOCK), 0)
    for sb in range(blocks_per_tile):
        r0 = sb * MOBA_BLOCK
        own = t * blocks_per_tile + sb
        onehot = jnp.where(lane - HEAD_DIM == own, 1.0, 0.0)
        past = blk < own
        for p in range(HEAD_PAIRS):
            if rest:
                rest.pop(0)()
            c0 = p * LANES
            kn = head_norm(zk[r0:r0 + MOBA_BLOCK, c0:c0 + LANES], gk_ref[...])
            km_s[p, pl.ds(own, 1), :] = jnp.mean(kn, axis=0, keepdims=True)
            qt = head_norm(zq[r0:r0 + MOBA_BLOCK, c0:c0 + LANES], gq_ref[...]).T
            vt = zv[r0:r0 + MOBA_BLOCK, c0:c0 + LANES].T
            for hh, kh in enumerate((kn, pltpu.roll(kn, HEAD_DIM, 1))):
                ka_ref[0, 2 * p + hh, r0:r0 + MOBA_BLOCK, :] = jnp.where(low, kh, onehot).astype(BF16)
                vh = vt[hh * HEAD_DIM:(hh + 1) * HEAD_DIM]
                vt_ref[0, 2 * p + hh, sb] = jnp.concatenate([vh, ones_rows], axis=0).astype(BF16)
                hmask = (feat >= hh * HEAD_DIM) & (feat < (hh + 1) * HEAD_DIM)
                gate = jnp.dot(km_s[p], jnp.where(hmask, qt, 0.0), precision=lax.Precision.HIGHEST,
                               preferred_element_type=F32)
                g = jnp.where(past, gate, -jnp.inf)
                sel = blk == own
                for _ in range(MOBA_TOPK):
                    mx = jnp.max(g, axis=0, keepdims=True)
                    idx = jnp.min(jnp.where(g == mx, blk, nblk), axis=0, keepdims=True)
                    hit = blk == idx
                    sel = sel | (hit & past)
                    g = jnp.where(hit, -jnp.inf, g)
                qa_ref[0, 2 * p + hh, sb] = jnp.concatenate(
                    [qt[hh * HEAD_DIM:(hh + 1) * HEAD_DIM] * (QK_SCALE * LOG2E), jnp.where(sel, 0.0, NEG),
                     jnp.zeros((LANES - HEAD_DIM - nblk, MOBA_BLOCK), F32)], axis=0).astype(BF16)
    for task in rest:
        task()


def _inproj(l, x, mod, g, w, gq2, gk2):
    bsz, seq, _ = x.shape
    nb = seq // MOBA_BLOCK
    bpt = TM // MOBA_BLOCK
    tok = lambda b, t: (b, t, 0)
    return pl.pallas_call(
        _inproj_kernel,
        grid=(bsz, seq // TM),
        in_specs=[
            pl.BlockSpec((1, TM, D_MODEL), tok),
            _mod_spec(l),
            _layer_spec(l, (1, D_MODEL)),
            _layer_spec(l, (D_MODEL, IN_WIDTH)),
            _layer_spec(l, (1, LANES)),
            _layer_spec(l, (1, LANES)),
        ],
        out_specs=[
            pl.BlockSpec((1, TM, CONV_CH), tok),
            pl.BlockSpec((1, TM, 2 * D_MODEL), tok),
            pl.BlockSpec((1, N_HEADS, bpt, LANES, MOBA_BLOCK), lambda b, t: (b, 0, t, 0, 0)),
            pl.BlockSpec((1, N_HEADS, TM, LANES), lambda b, t: (b, 0, t, 0)),
            pl.BlockSpec((1, N_HEADS, bpt, V_ROWS, MOBA_BLOCK), lambda b, t: (b, 0, t, 0, 0)),
        ],
        out_shape=[
            jax.ShapeDtypeStruct((bsz, seq, CONV_CH), BF16),
            jax.ShapeDtypeStruct((bsz, seq, 2 * D_MODEL), BF16),
            jax.ShapeDtypeStruct((bsz, N_HEADS, nb, LANES, MOBA_BLOCK), BF16),
            jax.ShapeDtypeStruct((bsz, N_HEADS, seq, LANES), BF16),
            jax.ShapeDtypeStruct((bsz, N_HEADS, nb, V_ROWS, MOBA_BLOCK), BF16),
        ],
        scratch_shapes=[pltpu.VMEM((HEAD_PAIRS, nb, LANES), F32)],
        compiler_params=_params("arbitrary", "arbitrary"),
        name="norm_inproj",
    )(x, mod, g, w, gq2, gk2)


BIAS_STRIP = 32


def _bias_tile_kernel(rb_ref, o_ref):
    dblk = pl.program_id(0)
    max_exact = NUM_BUCKETS // 2
    for k0 in range(0, MOBA_BLOCK, BIAS_STRIP):
        k = k0 + lax.broadcasted_iota(jnp.int32, (BIAS_STRIP, MOBA_BLOCK), 0)
        q = lax.broadcasted_iota(jnp.int32, (BIAS_STRIP, MOBA_BLOCK), 1)
        d = dblk * MOBA_BLOCK + q - k
        dist = jnp.maximum(d, 0)
        nf = jnp.maximum(dist, 1).astype(F32)
        large = max_exact + (jnp.log(nf / max_exact) / math.log(MAX_DISTANCE / max_exact)
                             * (NUM_BUCKETS - max_exact)).astype(jnp.int32)
        large = jnp.minimum(large, NUM_BUCKETS - 1)
        bucket = jnp.where(dist < max_exact, dist, large)
        for h in range(N_HEADS):
            val = jnp.zeros((BIAS_STRIP, MOBA_BLOCK), F32)
            for b in range(NUM_BUCKETS):
                val = jnp.where(bucket == b, rb_ref[h, b], val)
            o_ref[h, 0, k0:k0 + BIAS_STRIP, :] = jnp.where(d >= 0, val * LOG2E, NEG)


def _bias_tiles(rel_bias):
    return pl.pallas_call(
        _bias_tile_kernel,
        grid=(N_BIAS_TILES,),
        in_specs=[pl.BlockSpec(memory_space=pltpu.SMEM)],
        out_specs=pl.BlockSpec((N_HEADS, 1, MOBA_BLOCK, MOBA_BLOCK), lambda d: (0, d, 0, 0)),
        out_shape=jax.ShapeDtypeStruct((N_HEADS, N_BIAS_TILES, MOBA_BLOCK, MOBA_BLOCK), F32),
        compiler_params=_params("arbitrary"),
        name="t5_bias_tiles",
    )(rel_bias)


def _key_groups(first_q):
    visible = first_q + Q_CLASS
    n_far = max(first_q - (FAR_BLOCKS - 1), 0)
    groups = [(b, min(KB_GROUP, n_far - b), False) for b in range(0, n_far, KB_GROUP)]
    groups += [(b, min(KB_GROUP, visible - b), True) for b in range(n_far, visible, KB_GROUP)]
    return groups


def _attn_kernel(qa_ref, ka_ref, vt_ref, bias_ref, o_ref, *, first_q):
    step0 = first_q + pl.program_id(2) * Q_STEP
    groups = _key_groups(first_q)
    ahead = [None] * Q_STEP
    carries = [None] * Q_STEP
    for g in range(-1, len(groups)):
        for u in range(Q_STEP):
            q_augs = [qa_ref[0, hh, u] for hh in range(2)]
            scores, softmax_pv = _stream_ops(ka_ref, vt_ref, bias_ref, q_augs, step0 + u)
            if g < 0:
                carries[u] = tuple((jnp.full((1, MOBA_BLOCK), -jnp.inf, F32), jnp.zeros((V_ROWS, MOBA_BLOCK), F32))
                                   for _ in range(2))
                ahead[u] = scores(groups[0])
                continue
            ss = ahead[u]
            ahead[u] = scores(groups[g + 1]) if g + 1 < len(groups) else None
            carries[u] = softmax_pv(ss, groups[g], carries[u])
    for u in range(Q_STEP):
        outs = [acc[:HEAD_DIM] * (1.0 / acc[HEAD_DIM:HEAD_DIM + 1]) for _, acc in carries[u]]
        o_ref[0, u * MOBA_BLOCK:(u + 1) * MOBA_BLOCK, :] = jnp.concatenate(outs, axis=0).T.astype(BF16)


def _stream_ops(ka_ref, vt_ref, bias_ref, q_augs, i):
    def scores(group):
        b0, n, _ = group
        rows = [slice((b0 + u) * MOBA_BLOCK, (b0 + u + 1) * MOBA_BLOCK) for u in range(n)]
        return [[jnp.dot(ka_ref[0, hh, r, :], q_augs[hh], preferred_element_type=F32) for r in rows]
                for hh in range(2)]

    def softmax_pv(ss, group, carry):
        b0, n, near = group
        blocks = [b0 + u for u in range(n)]
        ps, stats = [], []
        for hh in range(2):
            m, _ = carry[hh]
            if near:
                s = [sv + bias_ref[hh, jnp.clip(i - b, 0, FAR_BLOCKS)] for sv, b in zip(ss[hh], blocks)]
                shift_by = 0.0
            else:
                s = ss[hh]
                shift_by = bias_ref[hh, FAR_BLOCKS, 0:1, 0:1]
            mx = jnp.max(functools.reduce(jnp.maximum, s), axis=0, keepdims=True)
            m_new = jnp.maximum(m, mx + shift_by)
            shift = m_new - shift_by
            alpha = jnp.exp2(m - m_new)
            stats.append((m_new, alpha))
            ps.append(jnp.concatenate([jnp.exp2(sv - shift).astype(BF16) for sv in s], axis=0))
        out = []
        for hh in range(2):
            m_new, alpha = stats[hh]
            vj = jnp.concatenate([vt_ref[0, hh, b] for b in blocks], axis=1)
            acc = alpha * carry[hh][1] + jnp.dot(vj, ps[hh], preferred_element_type=F32)
            out.append((m_new, acc))
        return tuple(out)

    return scores, softmax_pv


def _attn_instance(qa_ref, ka_ref, vt_ref, bias_ref, prev_ref, o_ref, *, first_q):
    del prev_ref
    _attn_kernel(qa_ref, ka_ref, vt_ref, bias_ref, o_ref, first_q=first_q)


def _attention(qa, ka, vt, bias):
    bsz, _, seq, _ = ka.shape
    nb = seq // MOBA_BLOCK
    out = jnp.zeros((bsz, seq, ATTN_WIDTH), BF16)
    for first in range(0, nb, Q_CLASS):
        keys = first + Q_CLASS
        out = pl.pallas_call(
            functools.partial(_attn_instance, first_q=first),
            grid=(bsz, HEAD_PAIRS, Q_CLASS // Q_STEP),
            in_specs=[
                pl.BlockSpec((1, 2, Q_STEP, LANES, MOBA_BLOCK),
                             lambda b, p, r, first=first: (b, p, first // Q_STEP + r, 0, 0)),
                pl.BlockSpec((1, 2, keys * MOBA_BLOCK, LANES), lambda b, p, r: (b, p, 0, 0)),
                pl.BlockSpec((1, 2, keys, V_ROWS, MOBA_BLOCK), lambda b, p, r: (b, p, 0, 0, 0)),
                pl.BlockSpec((2, N_BIAS_TILES, MOBA_BLOCK, MOBA_BLOCK), lambda b, p, r: (p, 0, 0, 0)),
                pl.BlockSpec(memory_space=pl.ANY),
            ],
            out_specs=pl.BlockSpec((1, Q_STEP * MOBA_BLOCK, LANES),
                                   lambda b, p, r, first=first: (b, first // Q_STEP + r, p)),
            out_shape=jax.ShapeDtypeStruct((bsz, seq, ATTN_WIDTH), BF16),
            input_output_aliases={4: 0},
            compiler_params=_params("parallel", "parallel", "arbitrary"),
            name=f"moba_attention_q{first}",
        )(qa, ka, vt, bias, out)
    return out


def _mixout_kernel(a_ref, halo_ref, zg_ref, o_ref, x_ref, mod_ref, cw_ref, cb_ref, lg_ref, lb_ref,
                   wpa_ref, wpb_ref, wo_ref, out_ref, a_ext, a_sh, y_buf):
    t = pl.program_id(1)
    a_ext[CONV_HALO:, :] = a_ref[0].astype(F32)
    a_ext[:CONV_HALO, :] = jnp.where(t > 0, halo_ref[0].astype(F32), 0.0)
    cw = cw_ref[...]
    cb = cb_ref[...]
    first = CONV_HALO - (CONV_KERNEL - 1)
    n_sh = a_sh.shape[1]
    for s in range(1, SUBLANES):
        a_sh[s - 1] = a_ext[s:s + n_sh, :]
    for r0 in range(0, TM, CONV_ROWS):
        acc = jnp.broadcast_to(cb, (CONV_ROWS, CONV_CH))
        for j in range(CONV_KERNEL):
            q, s = divmod(first + j, SUBLANES)
            lo = r0 + SUBLANES * q
            tap = a_ext[lo:lo + CONV_ROWS, :] if s == 0 else a_sh[s - 1, lo:lo + CONV_ROWS, :]
            acc = acc + cw[j:j + 1] * tap
        y_buf[r0:r0 + CONV_ROWS, :] = acc
    y = y_buf[...]
    mu = jnp.mean(y, axis=-1, keepdims=True)
    yc = y - mu
    var = jnp.mean(yc * yc, axis=-1, keepdims=True)
    yn = yc * lax.rsqrt(var + EPS) * lg_ref[...] + lb_ref[...]
    act = (yn * _sigmoid(yn)).astype(BF16)
    ya = jnp.dot(act, wpa_ref[...], preferred_element_type=F32)
    yb = jnp.dot(o_ref[0], wpb_ref[...], preferred_element_type=F32)
    ga = zg_ref[0, :, :D_MODEL].astype(F32)
    gb = zg_ref[0, :, D_MODEL:].astype(F32)
    m = (_sigmoid(ga) * ya + _sigmoid(gb) * yb).astype(BF16)
    y = jnp.dot(m, wo_ref[...], preferred_element_type=F32)
    out_ref[0] = x_ref[0] + mod_ref[0][2:3] * y


def _mixout(l, a, zg, o, x, mod, cw, cb, lg, lb, wpa, wpb, wo):
    bsz, seq, _ = x.shape
    per = TM // CONV_HALO
    tok = lambda b, t: (b, t, 0)
    return pl.pallas_call(
        _mixout_kernel,
        grid=(bsz, seq // TM),
        in_specs=[
            pl.BlockSpec((1, TM, CONV_CH), tok),
            pl.BlockSpec((1, CONV_HALO, CONV_CH), lambda b, t: (b, jnp.maximum(t * per - 1, 0), 0)),
            pl.BlockSpec((1, TM, 2 * D_MODEL), tok),
            pl.BlockSpec((1, TM, ATTN_WIDTH), tok),
            pl.BlockSpec((1, TM, D_MODEL), tok),
            _mod_spec(l),
            _layer_spec(l, (CONV_HALO, CONV_CH)),
            _layer_spec(l, (1, CONV_CH)),
            _layer_spec(l, (1, CONV_CH)),
            _layer_spec(l, (1, CONV_CH)),
            _layer_spec(l, (CONV_CH, D_MODEL)),
            _layer_spec(l, (ATTN_WIDTH, D_MODEL)),
            _layer_spec(l, (D_MODEL, D_MODEL)),
        ],
        out_specs=pl.BlockSpec((1, TM, D_MODEL), tok),
        out_shape=jax.ShapeDtypeStruct((bsz, seq, D_MODEL), F32),
        scratch_shapes=[pltpu.VMEM((TM + CONV_HALO, CONV_CH), F32),
                        pltpu.VMEM((SUBLANES - 1, TM + CONV_HALO - SUBLANES, CONV_CH), F32),
                        pltpu.VMEM((TM, CONV_CH), F32)],
        compiler_params=_params("parallel", "arbitrary"),
        name="mixer_out",
    )(a, a, zg, o, x, mod, cw, cb, lg, lb, wpa, wpb, wo)


def _ffn_kernel(x_ref, halo_ref, mod_ref, g_ref, wup_ref, fcw_ref, fcb_ref, wdn_ref, out_ref, act_buf):
    t = pl.program_id(1)
    mod = mod_ref[0]
    x = x_ref[0]
    xe = jnp.concatenate([halo_ref[0], x], axis=0)
    h = _mod_rmsnorm(xe, g_ref[...], mod[3:4], mod[4:5])
    row = lax.broadcasted_iota(jnp.int32, (FFN_HALO + TM, 1), 0)
    h = jnp.where((row >= FFN_HALO) | (t > 0), h, 0.0).astype(BF16)

    def conv(col):
        u = jnp.dot(h, wup_ref[:, col:col + FF_CHUNK], preferred_element_type=F32)
        w = fcw_ref[:, col:col + FF_CHUNK]
        y = w[2:3] * u + w[1:2] * pltpu.roll(u, 1, 0) + w[0:1] * pltpu.roll(u, 2, 0)
        return y[FFN_HALO:] + fcb_ref[:, col:col + FF_CHUNK]

    def down(g0, g1):
        return jnp.dot(act_buf[:, g0:g1], wdn_ref[g0:g1, :], preferred_element_type=F32)

    acc = None
    pending = None
    assert sum(FF_GROUPS) * FF_CHUNK == D_FF
    bounds = [FF_CHUNK * sum(FF_GROUPS[:k]) for k in range(len(FF_GROUPS) + 1)]
    for g0, g1 in zip(bounds[:-1], bounds[1:]):
        for c0 in range(g0, g1, FF_CHUNK):
            uv = conv(c0)
            ug = conv(D_FF + c0)
            act_buf[:, c0:c0 + FF_CHUNK] = (ug * _sigmoid(ug) * uv).astype(BF16)
            if pending is not None:
                part = down(*pending)
                acc = part if acc is None else acc + part
                pending = None
        pending = (g0, g1)
    part = down(*pending)
    acc = part if acc is None else acc + part
    out_ref[0] = x + mod[5:6] * acc


def _ffn(l, x, mod, g, wup, fcw, fcb, wdn):
    bsz, seq, _ = x.shape
    per = TM // FFN_HALO
    return pl.pallas_call(
        _ffn_kernel,
        grid=(bsz, seq // TM),
        in_specs=[
            pl.BlockSpec((1, TM, D_MODEL), lambda b, t: (b, t, 0)),
            pl.BlockSpec((1, FFN_HALO, D_MODEL), lambda b, t: (b, jnp.maximum(t * per - 1, 0), 0)),
            _mod_spec(l),
            _layer_spec(l, (1, D_MODEL)),
            _layer_spec(l, (D_MODEL, 2 * D_FF)),
            _layer_spec(l, (SUBLANES, 2 * D_FF)),
            _layer_spec(l, (1, 2 * D_FF)),
            _layer_spec(l, (D_FF, D_MODEL)),
        ],
        out_specs=pl.BlockSpec((1, TM, D_MODEL), lambda b, t: (b, t, 0)),
        out_shape=jax.ShapeDtypeStruct((bsz, seq, D_MODEL), F32),
        scratch_shapes=[pltpu.VMEM((TM, D_FF), BF16)],
        compiler_params=_params("parallel", "arbitrary"),
        name="conv_ffn",
    )(x, x, mod, g, wup, fcw, fcb, wdn)


def kernel(x, c, ada_w, ada_b, norm1_g, w_in, conv_w, conv_b, conv_ln_g, conv_ln_b, q_norm_g, k_norm_g,
           w_pa, w_pb, w_o, norm2_g, w_up, ffn_conv_w, ffn_conv_b, w_down, rel_bias):
    mod = _modulation(c, ada_w, ada_b)
    bias = _bias_tiles(rel_bias)
    row = lambda p: p[:, None, :]
    cw = jnp.pad(conv_w, ((0, 0), (0, CONV_HALO - CONV_KERNEL), (0, 0)))
    fcw = jnp.pad(ffn_conv_w, ((0, 0), (0, SUBLANES - FFN_CONV), (0, 0)))
    gq2, gk2 = row(jnp.tile(q_norm_g, (1, 2))), row(jnp.tile(k_norm_g, (1, 2)))
    w_in, w_pa, w_pb, w_o, w_up, w_down = (w.astype(BF16) for w in (w_in, w_pa, w_pb, w_o, w_up, w_down))
    for l in range(DEPTH):
        a, zg, qa, ka, vt = _inproj(l, x, mod, row(norm1_g), w_in, gq2, gk2)
        o = _attention(qa, ka, vt, bias)
        x = _mixout(l, a, zg, o, x, mod, cw, row(conv_b), row(conv_ln_g), row(conv_ln_b), w_pa, w_pb, w_o)
        x = _ffn(l, x, mod, row(norm2_g), w_up, fcw, row(ffn_conv_b), w_down)
    return x
```

```python
import functools
import math

import jax
import jax.numpy as jnp
from jax import lax
from jax.experimental import pallas as pl
from jax.experimental.pallas import tpu as pltpu

F32 = jnp.float32
BF16 = jnp.bfloat16

D_MODEL = 1024
DEPTH = 4
CONV_CH = 512
CONV_KERNEL = 31
HEAD_DIM = 64
N_HEADS = 8
ATTN_WIDTH = 512
MOBA_BLOCK = 256
MOBA_TOPK = 3
NUM_BUCKETS = 32
MAX_DISTANCE = 2048
D_FF = 2816
FFN_CONV = 3
EPS = 1e-6
IN_WIDTH = 2 * CONV_CH + 3 * ATTN_WIDTH + 2 * D_MODEL
Q_COL = 2 * CONV_CH
K_COL = Q_COL + ATTN_WIDTH
V_COL = K_COL + ATTN_WIDTH
GATE_COL = V_COL + ATTN_WIDTH

LANES = 128
SUBLANES = 8
HEAD_PAIRS = N_HEADS // 2
LOG2E = math.log2(math.e)
QK_SCALE = HEAD_DIM ** -0.5
NEG = -30000.0
V_ROWS = HEAD_DIM + 16
FAR_BLOCKS = -(-(MAX_DISTANCE + MOBA_BLOCK) // MOBA_BLOCK)
N_BIAS_TILES = FAR_BLOCKS + 1
CONV_HALO = 32
CONV_ROWS = 32
FFN_HALO = SUBLANES
VMEM_LIMIT = 56 * 1024 * 1024

TM = 512
FF_CHUNK = 256
FF_GROUPS = (4, 4, 2, 1)
KB_GROUP = 4
Q_CLASS = 4


def _sigmoid(x):
    return 1.0 / (1.0 + jnp.exp(-x))


def _params(*sem):
    return pltpu.CompilerParams(dimension_semantics=sem, vmem_limit_bytes=VMEM_LIMIT)


def _layer_spec(l, shape):
    nd = len(shape)
    return pl.BlockSpec((None,) + shape, lambda *_: (l,) + (0,) * nd, pipeline_mode=pl.Buffered(1))


def _mod_spec(l):
    return pl.BlockSpec((None, 1, SUBLANES, D_MODEL), lambda b, t: (l, b, 0, 0))


def _mod_kernel(c_ref, w_ref, b_ref, o_ref):
    c = c_ref[...]
    cond = c * _sigmoid(c)
    o_ref[0] = jnp.dot(cond, w_ref[0], precision=lax.Precision.HIGHEST,
                       preferred_element_type=F32) + b_ref[0]


def _modulation(c, ada_w, ada_b):
    bsz = c.shape[0]
    c8 = jnp.pad(c, ((0, SUBLANES - bsz), (0, 0)))
    tn = 1536
    n = 6 * D_MODEL
    out = pl.pallas_call(
        _mod_kernel,
        grid=(DEPTH, n // tn),
        in_specs=[
            pl.BlockSpec((SUBLANES, D_MODEL), lambda l, j: (0, 0)),
            pl.BlockSpec((1, D_MODEL, tn), lambda l, j: (l, 0, j)),
            pl.BlockSpec((1, 1, tn), lambda l, j: (l, 0, j)),
        ],
        out_specs=pl.BlockSpec((1, SUBLANES, tn), lambda l, j: (l, 0, j)),
        out_shape=jax.ShapeDtypeStruct((DEPTH, SUBLANES, n), F32),
        compiler_params=_params("arbitrary", "arbitrary"),
        name="adaln_modulation",
    )(c8, ada_w, ada_b.reshape(DEPTH, 1, n))
    mod = out[:, :bsz].reshape(DEPTH, bsz, 6, D_MODEL)
    return jnp.pad(mod, ((0, 0), (0, 0), (0, SUBLANES - 6), (0, 0)))


def _mod_rmsnorm(x, g, shift, scale):
    ms = jnp.mean(x * x, axis=-1, keepdims=True)
    return (x * lax.rsqrt(ms + EPS) * g) * (1.0 + scale) + shift


def _inproj_kernel(x_ref, mod_ref, g_ref, w_ref, gq_ref, gk_ref, a_ref, zg_ref, qa_ref, ka_ref, vt_ref, km_s):
    t = pl.program_id(1)

    @pl.when(t == 0)
    def _():
        km_s[...] = jnp.zeros(km_s.shape, F32)

    mod = mod_ref[0]
    h = _mod_rmsnorm(x_ref[0], g_ref[...], mod[0:1], mod[1:2]).astype(BF16)
    tn = 512

    def proj(col):
        return jnp.dot(h, w_ref[:, col:col + tn], preferred_element_type=F32)

    zq, zk, zv = proj(Q_COL), proj(K_COL), proj(V_COL)

    def glu():
        a_ref[0] = (proj(0) * _sigmoid(proj(CONV_CH))).astype(BF16)

    def gate_cols(n0):
        zg_ref[0, :, n0:n0 + tn] = proj(GATE_COL + n0).astype(BF16)

    rest = [glu] + [functools.partial(gate_cols, n0) for n0 in range(0, 2 * D_MODEL, tn)]
    lane = lax.broadcasted_iota(jnp.int32, (MOBA_BLOCK, LANES), 1)
    low = lane < HEAD_DIM

    def head_norm(tv, g):
        t2 = tv * tv
        s0 = jnp.sum(jnp.where(low, t2, 0.0), axis=-1, keepdims=True)
        s1 = jnp.sum(jnp.where(low, 0.0, t2), axis=-1, keepdims=True)
        ms = jnp.where(low, s0, s1) * (1.0 / HEAD_DIM)
        return tv * lax.rsqrt(ms + EPS) * g

    blocks_per_tile = TM // MOBA_BLOCK
    nblk = km_s.shape[1]
    ones_rows = jnp.where(lax.broadcasted_iota(jnp.int32, (V_ROWS - HEAD_DIM, MOBA_BLOCK), 0) == 0, 1.0, 0.0)
    feat = lax.broadcasted_iota(jnp.int32, (LANES, MOBA_BLOCK), 0)
    blk = lax.broadcasted_iota(jnp.int32, (nblk, MOBA_BLOCK), 0)
    for sb in range(blocks_per_tile):
        r0 = sb * MOBA_BLOCK
        own = t * blocks_per_tile + sb
        onehot = jnp.where(lane - HEAD_DIM == own, 1.0, 0.0)
        past = blk < own
        for p in range(HEAD_PAIRS):
            if rest:
                rest.pop(0)()
            c0 = p * LANES
            kn = head_norm(zk[r0:r0 + MOBA_BLOCK, c0:c0 + LANES], gk_ref[...])
            km_s[p, pl.ds(own, 1), :] = jnp.mean(kn, axis=0, keepdims=True)
            qt = head_norm(zq[r0:r0 + MOBA_BLOCK, c0:c0 + LANES], gq_ref[...]).T
            vt = zv[r0:r0 + MOBA_BLOCK, c0:c0 + LANES].T
            for hh, kh in enumerate((kn, pltpu.roll(kn, HEAD_DIM, 1))):
                ka_ref[0, 2 * p + hh, r0:r0 + MOBA_BLOCK, :] = jnp.where(low, kh, onehot).astype(BF16)
                vh = vt[hh * HEAD_DIM:(hh + 1) * HEAD_DIM]
                vt_ref[0, 2 * p + hh, sb] = jnp.concatenate([vh, ones_rows], axis=0).astype(BF16)
                hmask = (feat >= hh * HEAD_DIM) & (feat < (hh + 1) * HEAD_DIM)
                gate = jnp.dot(km_s[p], jnp.where(hmask, qt, 0.0), precision=lax.Precision.HIGHEST,
                               preferred_element_type=F32)
                g = jnp.where(past, gate, -jnp.inf)
                sel = blk == own
                for _ in range(MOBA_TOPK):
                    mx = jnp.max(g, axis=0, keepdims=True)
                    idx = jnp.min(jnp.where(g == mx, blk, nblk), axis=0, keepdims=True)
                    hit = blk == idx
                    sel = sel | (hit & past)
                    g = jnp.where(hit, -jnp.inf, g)
                qa_ref[0, 2 * p + hh, sb] = jnp.concatenate(
                    [qt[hh * HEAD_DIM:(hh + 1) * HEAD_DIM] * (QK_SCALE * LOG2E), jnp.where(sel, 0.0, NEG),
                     jnp.zeros((LANES - HEAD_DIM - nblk, MOBA_BLOCK), F32)], axis=0).astype(BF16)
    for task in rest:
        task()


def _inproj(l, x, mod, g, w, gq2, gk2):
    bsz, seq, _ = x.shape
    nb = seq // MOBA_BLOCK
    bpt = TM // MOBA_BLOCK
    tok = lambda b, t: (b, t, 0)
    return pl.pallas_call(
        _inproj_kernel,
        grid=(bsz, seq // TM),
        in_specs=[
            pl.BlockSpec((1, TM, D_MODEL), tok),
            _mod_spec(l),
            _layer_spec(l, (1, D_MODEL)),
            _layer_spec(l, (D_MODEL, IN_WIDTH)),
            _layer_spec(l, (1, LANES)),
            _layer_spec(l, (1, LANES)),
        ],
        out_specs=[
            pl.BlockSpec((1, TM, CONV_CH), tok),
            pl.BlockSpec((1, TM, 2 * D_MODEL), tok),
            pl.BlockSpec((1, N_HEADS, bpt, LANES, MOBA_BLOCK), lambda b, t: (b, 0, t, 0, 0)),
            pl.BlockSpec((1, N_HEADS, TM, LANES), lambda b, t: (b, 0, t, 0)),
            pl.BlockSpec((1, N_HEADS, bpt, V_ROWS, MOBA_BLOCK), lambda b, t: (b, 0, t, 0, 0)),
        ],
        out_shape=[
            jax.ShapeDtypeStruct((bsz, seq, CONV_CH), BF16),
            jax.ShapeDtypeStruct((bsz, seq, 2 * D_MODEL), BF16),
            jax.ShapeDtypeStruct((bsz, N_HEADS, nb, LANES, MOBA_BLOCK), BF16),
            jax.ShapeDtypeStruct((bsz, N_HEADS, seq, LANES), BF16),
            jax.ShapeDtypeStruct((bsz, N_HEADS, nb, V_ROWS, MOBA_BLOCK), BF16),
        ],
        scratch_shapes=[pltpu.VMEM((HEAD_PAIRS, nb, LANES), F32)],
        compiler_params=_params("arbitrary", "arbitrary"),
        name="norm_inproj",
    )(x, mod, g, w, gq2, gk2)


BIAS_STRIP = 32


def _bias_tile_kernel(rb_ref, o_ref):
    dblk = pl.program_id(0)
    max_exact = NUM_BUCKETS // 2
    for k0 in range(0, MOBA_BLOCK, BIAS_STRIP):
        k = k0 + lax.broadcasted_iota(jnp.int32, (BIAS_STRIP, MOBA_BLOCK), 0)
        q = lax.broadcasted_iota(jnp.int32, (BIAS_STRIP, MOBA_BLOCK), 1)
        d = dblk * MOBA_BLOCK + q - k
        dist = jnp.maximum(d, 0)
        nf = jnp.maximum(dist, 1).astype(F32)
        large = max_exact + (jnp.log(nf / max_exact) / math.log(MAX_DISTANCE / max_exact)
                             * (NUM_BUCKETS - max_exact)).astype(jnp.int32)
        large = jnp.minimum(large, NUM_BUCKETS - 1)
        bucket = jnp.where(dist < max_exact, dist, large)
        for h in range(N_HEADS):
            val = jnp.zeros((BIAS_STRIP, MOBA_BLOCK), F32)
            for b in range(NUM_BUCKETS):
                val = jnp.where(bucket == b, rb_ref[h, b], val)
            o_ref[h, 0, k0:k0 + BIAS_STRIP, :] = jnp.where(d >= 0, val * LOG2E, NEG)


def _bias_tiles(rel_bias):
    return pl.pallas_call(
        _bias_tile_kernel,
        grid=(N_BIAS_TILES,),
        in_specs=[pl.BlockSpec(memory_space=pltpu.SMEM)],
        out_specs=pl.BlockSpec((N_HEADS, 1, MOBA_BLOCK, MOBA_BLOCK), lambda d: (0, d, 0, 0)),
        out_shape=jax.ShapeDtypeStruct((N_HEADS, N_BIAS_TILES, MOBA_BLOCK, MOBA_BLOCK), F32),
        compiler_params=_params("arbitrary"),
        name="t5_bias_tiles",
    )(rel_bias)


def _key_groups(i):
    n_far = max(i - (FAR_BLOCKS - 1), 0)
    groups = []
    for lo, hi, near in ((0, n_far, False), (n_far, i + 1, True)):
        parts = -(-(hi - lo) // KB_GROUP)
        for k in range(parts):
            b0 = lo + (hi - lo) * k // parts
            b1 = lo + (hi - lo) * (k + 1) // parts
            groups.append((b0, b1 - b0, near))
    return groups


def _attn_kernel(qa_ref, ka_ref, vt_ref, bias_ref, o_ref, *, first_q):
    groups = [_key_groups(first_q + u) for u in range(Q_CLASS)]
    ahead = [None] * Q_CLASS
    carries = [None] * Q_CLASS
    for g in range(-1, max(len(gs) for gs in groups)):
        for u in range(Q_CLASS):
            if g >= len(groups[u]):
                continue
            q_augs = [qa_ref[0, hh, u] for hh in range(2)]
            scores, softmax_pv = _stream_ops(ka_ref, vt_ref, bias_ref, q_augs, first_q + u)
            if g < 0:
                carries[u] = tuple((jnp.full((1, MOBA_BLOCK), -jnp.inf, F32), jnp.zeros((V_ROWS, MOBA_BLOCK), F32))
                                   for _ in range(2))
                ahead[u] = scores(groups[u][0])
                continue
            ss = ahead[u]
            ahead[u] = scores(groups[u][g + 1]) if g + 1 < len(groups[u]) else None
            carries[u] = softmax_pv(ss, groups[u][g], carries[u])
    for u in range(Q_CLASS):
        outs = [acc[:HEAD_DIM] * (1.0 / acc[HEAD_DIM:HEAD_DIM + 1]) for _, acc in carries[u]]
        o_ref[0, u * MOBA_BLOCK:(u + 1) * MOBA_BLOCK, :] = jnp.concatenate(outs, axis=0).T.astype(BF16)


def _stream_ops(ka_ref, vt_ref, bias_ref, q_augs, i):
    def scores(group):
        b0, n, _ = group
        rows = [slice((b0 + u) * MOBA_BLOCK, (b0 + u + 1) * MOBA_BLOCK) for u in range(n)]
        return [[jnp.dot(ka_ref[0, hh, r, :], q_augs[hh], preferred_element_type=F32) for r in rows]
                for hh in range(2)]

    def softmax_pv(ss, group, carry):
        b0, n, near = group
        blocks = [b0 + u for u in range(n)]
        ps, stats = [], []
        for hh in range(2):
            m, _ = carry[hh]
            if near:
                s = [sv + bias_ref[hh, min(i - b, FAR_BLOCKS)] for sv, b in zip(ss[hh], blocks)]
                shift_by = 0.0
            else:
                s = ss[hh]
                shift_by = bias_ref[hh, FAR_BLOCKS, 0:1, 0:1]
            mx = jnp.max(functools.reduce(jnp.maximum, s), axis=0, keepdims=True)
            m_new = jnp.maximum(m, mx + shift_by)
            shift = m_new - shift_by
            alpha = jnp.exp2(m - m_new)
            stats.append((m_new, alpha))
            ps.append(jnp.concatenate([jnp.exp2(sv - shift).astype(BF16) for sv in s], axis=0))
        out = []
        for hh in range(2):
            m_new, alpha = stats[hh]
            vj = jnp.concatenate([vt_ref[0, hh, b] for b in blocks], axis=1)
            acc = alpha * carry[hh][1] + jnp.dot(vj, ps[hh], preferred_element_type=F32)
            out.append((m_new, acc))
        return tuple(out)

    return scores, softmax_pv


def _attn_instance(qa_ref, ka_ref, vt_ref, bias_ref, prev_ref, o_ref, *, first_q):
    del prev_ref
    _attn_kernel(qa_ref, ka_ref, vt_ref, bias_ref, o_ref, first_q=first_q)


def _attention(qa, ka, vt, bias):
    bsz, _, seq, _ = ka.shape
    nb = seq // MOBA_BLOCK
    out = jnp.zeros((bsz, seq, ATTN_WIDTH), BF16)
    for first in range(0, nb, Q_CLASS):
        keys = first + Q_CLASS
        out = pl.pallas_call(
            functools.partial(_attn_instance, first_q=first),
            grid=(bsz, HEAD_PAIRS),
            in_specs=[
                pl.BlockSpec((1, 2, Q_CLASS, LANES, MOBA_BLOCK),
                             lambda b, p, first=first: (b, p, first // Q_CLASS, 0, 0)),
                pl.BlockSpec((1, 2, keys * MOBA_BLOCK, LANES), lambda b, p: (b, p, 0, 0)),
                pl.BlockSpec((1, 2, keys, V_ROWS, MOBA_BLOCK), lambda b, p: (b, p, 0, 0, 0)),
                pl.BlockSpec((2, N_BIAS_TILES, MOBA_BLOCK, MOBA_BLOCK), lambda b, p: (p, 0, 0, 0)),
                pl.BlockSpec(memory_space=pl.ANY),
            ],
            out_specs=pl.BlockSpec((1, Q_CLASS * MOBA_BLOCK, LANES),
                                   lambda b, p, first=first: (b, first // Q_CLASS, p)),
            out_shape=jax.ShapeDtypeStruct((bsz, seq, ATTN_WIDTH), BF16),
            input_output_aliases={4: 0},
            compiler_params=_params("parallel", "arbitrary"),
            name=f"moba_attention_q{first}",
        )(qa, ka, vt, bias, out)
    return out


def _mixout_kernel(a_ref, halo_ref, zg_ref, o_ref, x_ref, mod_ref, cw_ref, cb_ref, lg_ref, lb_ref,
                   wpa_ref, wpb_ref, wo_ref, out_ref, a_ext, a_sh, y_buf):
    t = pl.program_id(1)
    a_ext[CONV_HALO:, :] = a_ref[0].astype(F32)
    a_ext[:CONV_HALO, :] = jnp.where(t > 0, halo_ref[0].astype(F32), 0.0)
    cw = cw_ref[...]
    cb = cb_ref[...]
    first = CONV_HALO - (CONV_KERNEL - 1)
    n_sh = a_sh.shape[1]
    for s in range(1, SUBLANES):
        a_sh[s - 1] = a_ext[s:s + n_sh, :]
    for r0 in range(0, TM, CONV_ROWS):
        acc = jnp.broadcast_to(cb, (CONV_ROWS, CONV_CH))
        for j in range(CONV_KERNEL):
            q, s = divmod(first + j, SUBLANES)
            lo = r0 + SUBLANES * q
            tap = a_ext[lo:lo + CONV_ROWS, :] if s == 0 else a_sh[s - 1, lo:lo + CONV_ROWS, :]
            acc = acc + cw[j:j + 1] * tap
        y_buf[r0:r0 + CONV_ROWS, :] = acc
    y = y_buf[...]
    mu = jnp.mean(y, axis=-1, keepdims=True)
    yc = y - mu
    var = jnp.mean(yc * yc, axis=-1, keepdims=True)
    yn = yc * lax.rsqrt(var + EPS) * lg_ref[...] + lb_ref[...]
    act = (yn * _sigmoid(yn)).astype(BF16)
    ya = jnp.dot(act, wpa_ref[...], preferred_element_type=F32)
    yb = jnp.dot(o_ref[0], wpb_ref[...], preferred_element_type=F32)
    ga = zg_ref[0, :, :D_MODEL].astype(F32)
    gb = zg_ref[0, :, D_MODEL:].astype(F32)
    m = (_sigmoid(ga) * ya + _sigmoid(gb) * yb).astype(BF16)
    y = jnp.dot(m, wo_ref[...], preferred_element_type=F32)
    out_ref[0] = x_ref[0] + mod_ref[0][2:3] * y


def _mixout(l, a, zg, o, x, mod, cw, cb, lg, lb, wpa, wpb, wo):
    bsz, seq, _ = x.shape
    per = TM // CONV_HALO
    tok = lambda b, t: (b, t, 0)
    return pl.pallas_call(
        _mixout_kernel,
        grid=(bsz, seq // TM),
        in_specs=[
            pl.BlockSpec((1, TM, CONV_CH), tok),
            pl.BlockSpec((1, CONV_HALO, CONV_CH), lambda b, t: (b, jnp.maximum(t * per - 1, 0), 0)),
            pl.BlockSpec((1, TM, 2 * D_MODEL), tok),
            pl.BlockSpec((1, TM, ATTN_WIDTH), tok),
            pl.BlockSpec((1, TM, D_MODEL), tok),
            _mod_spec(l),
            _layer_spec(l, (CONV_HALO, CONV_CH)),
            _layer_spec(l, (1, CONV_CH)),
            _layer_spec(l, (1, CONV_CH)),
            _layer_spec(l, (1, CONV_CH)),
            _layer_spec(l, (CONV_CH, D_MODEL)),
            _layer_spec(l, (ATTN_WIDTH, D_MODEL)),
            _layer_spec(l, (D_MODEL, D_MODEL)),
        ],
        out_specs=pl.BlockSpec((1, TM, D_MODEL), tok),
        out_shape=jax.ShapeDtypeStruct((bsz, seq, D_MODEL), F32),
        scratch_shapes=[pltpu.VMEM((TM + CONV_HALO, CONV_CH), F32),
                        pltpu.VMEM((SUBLANES - 1, TM + CONV_HALO - SUBLANES, CONV_CH), F32),
                        pltpu.VMEM((TM, CONV_CH), F32)],
        compiler_params=_params("parallel", "arbitrary"),
        name="mixer_out",
    )(a, a, zg, o, x, mod, cw, cb, lg, lb, wpa, wpb, wo)


def _ffn_kernel(x_ref, halo_ref, mod_ref, g_ref, wup_ref, fcw_ref, fcb_ref, wdn_ref, out_ref, act_buf):
    t = pl.program_id(1)
    mod = mod_ref[0]
    x = x_ref[0]
    xe = jnp.concatenate([halo_ref[0], x], axis=0)
    h = _mod_rmsnorm(xe, g_ref[...], mod[3:4], mod[4:5])
    row = lax.broadcasted_iota(jnp.int32, (FFN_HALO + TM, 1), 0)
    h = jnp.where((row >= FFN_HALO) | (t > 0), h, 0.0).astype(BF16)

    def conv(col):
        u = jnp.dot(h, wup_ref[:, col:col + FF_CHUNK], preferred_element_type=F32)
        w = fcw_ref[:, col:col + FF_CHUNK]
        y = w[2:3] * u + w[1:2] * pltpu.roll(u, 1, 0) + w[0:1] * pltpu.roll(u, 2, 0)
        return y[FFN_HALO:] + fcb_ref[:, col:col + FF_CHUNK]

    def down(g0, g1):
        return jnp.dot(act_buf[:, g0:g1], wdn_ref[g0:g1, :], preferred_element_type=F32)

    acc = None
    pending = None
    assert sum(FF_GROUPS) * FF_CHUNK == D_FF
    bounds = [FF_CHUNK * sum(FF_GROUPS[:k]) for k in range(len(FF_GROUPS) + 1)]
    for g0, g1 in zip(bounds[:-1], bounds[1:]):
        for c0 in range(g0, g1, FF_CHUNK):
            uv = conv(c0)
            ug = conv(D_FF + c0)
            act_buf[:, c0:c0 + FF_CHUNK] = (ug * _sigmoid(ug) * uv).astype(BF16)
            if pending is not None:
                part = down(*pending)
                acc = part if acc is None else acc + part
                pending = None
        pending = (g0, g1)
    part = down(*pending)
    acc = part if acc is None else acc + part
    out_ref[0] = x + mod[5:6] * acc


def _ffn(l, x, mod, g, wup, fcw, fcb, wdn):
    bsz, seq, _ = x.shape
    per = TM // FFN_HALO
    return pl.pallas_call(
        _ffn_kernel,
        grid=(bsz, seq // TM),
        in_specs=[
            pl.BlockSpec((1, TM, D_MODEL), lambda b, t: (b, t, 0)),
            pl.BlockSpec((1, FFN_HALO, D_MODEL), lambda b, t: (b, jnp.maximum(t * per - 1, 0), 0)),
            _mod_spec(l),
            _layer_spec(l, (1, D_MODEL)),
            _layer_spec(l, (D_MODEL, 2 * D_FF)),
            _layer_spec(l, (SUBLANES, 2 * D_FF)),
            _layer_spec(l, (1, 2 * D_FF)),
            _layer_spec(l, (D_FF, D_MODEL)),
        ],
        out_specs=pl.BlockSpec((1, TM, D_MODEL), lambda b, t: (b, t, 0)),
        out_shape=jax.ShapeDtypeStruct((bsz, seq, D_MODEL), F32),
        scratch_shapes=[pltpu.VMEM((TM, D_FF), BF16)],
        compiler_params=_params("parallel", "arbitrary"),
        name="conv_ffn",
    )(x, x, mod, g, wup, fcw, fcb, wdn)


def kernel(x, c, ada_w, ada_b, norm1_g, w_in, conv_w, conv_b, conv_ln_g, conv_ln_b, q_norm_g, k_norm_g,
           w_pa, w_pb, w_o, norm2_g, w_up, ffn_conv_w, ffn_conv_b, w_down, rel_bias):
    mod = _modulation(c, ada_w, ada_b)
    bias = _bias_tiles(rel_bias)
    row = lambda p: p[:, None, :]
    cw = jnp.pad(conv_w, ((0, 0), (0, CONV_HALO - CONV_KERNEL), (0, 0)))
    fcw = jnp.pad(ffn_conv_w, ((0, 0), (0, SUBLANES - FFN_CONV), (0, 0)))
    gq2, gk2 = row(jnp.tile(q_norm_g, (1, 2))), row(jnp.tile(k_norm_g, (1, 2)))
    w_in, w_pa, w_pb, w_o, w_up, w_down = (w.astype(BF16) for w in (w_in, w_pa, w_pb, w_o, w_up, w_down))
    for l in range(DEPTH):
        a, zg, qa, ka, vt = _inproj(l, x, mod, row(norm1_g), w_in, gq2, gk2)
        o = _attention(qa, ka, vt, bias)
        x = _mixout(l, a, zg, o, x, mod, cw, row(conv_b), row(conv_ln_g), row(conv_ln_b), w_pa, w_pb, w_o)
        x = _ffn(l, x, mod, row(norm2_g), w_up, fcw, row(ffn_conv_b), w_down)
    return x
```

```python
import functools
import math

import jax
import jax.numpy as jnp
from jax import lax
from jax.experimental import pallas as pl
from jax.experimental.pallas import tpu as pltpu

F32 = jnp.float32
BF16 = jnp.bfloat16

D_MODEL = 1024
DEPTH = 4
CONV_CH = 512
CONV_KERNEL = 31
HEAD_DIM = 64
N_HEADS = 8
ATTN_WIDTH = 512
MOBA_BLOCK = 256
MOBA_TOPK = 3
NUM_BUCKETS = 32
MAX_DISTANCE = 2048
D_FF = 2816
FFN_CONV = 3
EPS = 1e-6
IN_WIDTH = 2 * CONV_CH + 3 * ATTN_WIDTH + 2 * D_MODEL
Q_COL = 2 * CONV_CH
K_COL = Q_COL + ATTN_WIDTH
V_COL = K_COL + ATTN_WIDTH
GATE_COL = V_COL + ATTN_WIDTH

LANES = 128
SUBLANES = 8
HEAD_PAIRS = N_HEADS // 2
LOG2E = math.log2(math.e)
QK_SCALE = HEAD_DIM ** -0.5
NEG = -1e30
V_ROWS = HEAD_DIM + 16
FAR_BLOCKS = -(-(MAX_DISTANCE + MOBA_BLOCK) // MOBA_BLOCK)
N_BIAS_TILES = FAR_BLOCKS + 1
CONV_HALO = 32
CONV_ROWS = 32
FFN_HALO = SUBLANES
VMEM_LIMIT = 56 * 1024 * 1024

TM = 512
FF_CHUNK = 256
FF_GROUPS = (4, 4, 2, 1)
KB_GROUP = 6
Q_CLASS = 4


def _sigmoid(x):
    return 1.0 / (1.0 + jnp.exp(-x))


def _params(*sem):
    return pltpu.CompilerParams(dimension_semantics=sem, vmem_limit_bytes=VMEM_LIMIT)


def _layer_spec(l, shape):
    nd = len(shape)
    return pl.BlockSpec((None,) + shape, lambda *_: (l,) + (0,) * nd, pipeline_mode=pl.Buffered(1))


def _mod_spec(l):
    return pl.BlockSpec((None, 1, SUBLANES, D_MODEL), lambda b, t: (l, b, 0, 0))


def _mod_kernel(c_ref, w_ref, b_ref, o_ref):
    c = c_ref[...]
    cond = c * _sigmoid(c)
    o_ref[0] = jnp.dot(cond, w_ref[0], precision=lax.Precision.HIGHEST,
                       preferred_element_type=F32) + b_ref[0]


def _modulation(c, ada_w, ada_b):
    bsz = c.shape[0]
    c8 = jnp.pad(c, ((0, SUBLANES - bsz), (0, 0)))
    tn = 1536
    n = 6 * D_MODEL
    out = pl.pallas_call(
        _mod_kernel,
        grid=(DEPTH, n // tn),
        in_specs=[
            pl.BlockSpec((SUBLANES, D_MODEL), lambda l, j: (0, 0)),
            pl.BlockSpec((1, D_MODEL, tn), lambda l, j: (l, 0, j)),
            pl.BlockSpec((1, 1, tn), lambda l, j: (l, 0, j)),
        ],
        out_specs=pl.BlockSpec((1, SUBLANES, tn), lambda l, j: (l, 0, j)),
        out_shape=jax.ShapeDtypeStruct((DEPTH, SUBLANES, n), F32),
        compiler_params=_params("arbitrary", "arbitrary"),
        name="adaln_modulation",
    )(c8, ada_w, ada_b.reshape(DEPTH, 1, n))
    mod = out[:, :bsz].reshape(DEPTH, bsz, 6, D_MODEL)
    return jnp.pad(mod, ((0, 0), (0, 0), (0, SUBLANES - 6), (0, 0)))


def _mod_rmsnorm(x, g, shift, scale):
    ms = jnp.mean(x * x, axis=-1, keepdims=True)
    return (x * lax.rsqrt(ms + EPS) * g) * (1.0 + scale) + shift


def _inproj_kernel(x_ref, mod_ref, g_ref, w_ref, gq_ref, gk_ref, a_ref, zg_ref, qa_ref, ka_ref, vt_ref, km_s):
    t = pl.program_id(1)

    @pl.when(t == 0)
    def _():
        km_s[...] = jnp.zeros(km_s.shape, F32)

    mod = mod_ref[0]
    h = _mod_rmsnorm(x_ref[0], g_ref[...], mod[0:1], mod[1:2]).astype(BF16)
    tn = 512

    def proj(col):
        return jnp.dot(h, w_ref[:, col:col + tn], preferred_element_type=F32)

    zq, zk, zv = proj(Q_COL), proj(K_COL), proj(V_COL)

    def glu():
        a_ref[0] = (proj(0) * _sigmoid(proj(CONV_CH))).astype(BF16)

    def gate_cols(n0):
        zg_ref[0, :, n0:n0 + tn] = proj(GATE_COL + n0).astype(BF16)

    rest = [glu] + [functools.partial(gate_cols, n0) for n0 in range(0, 2 * D_MODEL, tn)]
    lane = lax.broadcasted_iota(jnp.int32, (MOBA_BLOCK, LANES), 1)
    low = lane < HEAD_DIM

    def head_norm(tv, g):
        t2 = tv * tv
        s0 = jnp.sum(jnp.where(low, t2, 0.0), axis=-1, keepdims=True)
        s1 = jnp.sum(jnp.where(low, 0.0, t2), axis=-1, keepdims=True)
        ms = jnp.where(low, s0, s1) * (1.0 / HEAD_DIM)
        return tv * lax.rsqrt(ms + EPS) * g

    blocks_per_tile = TM // MOBA_BLOCK
    nblk = km_s.shape[1]
    ones_rows = jnp.where(lax.broadcasted_iota(jnp.int32, (V_ROWS - HEAD_DIM, MOBA_BLOCK), 0) == 0, 1.0, 0.0)
    feat = lax.broadcasted_iota(jnp.int32, (LANES, MOBA_BLOCK), 0)
    blk = lax.broadcasted_iota(jnp.int32, (nblk, MOBA_BLOCK), 0)
    for sb in range(blocks_per_tile):
        r0 = sb * MOBA_BLOCK
        own = t * blocks_per_tile + sb
        onehot = jnp.where(lane - HEAD_DIM == own, 1.0, 0.0)
        past = blk < own
        for p in range(HEAD_PAIRS):
            if rest:
                rest.pop(0)()
            c0 = p * LANES
            kn = head_norm(zk[r0:r0 + MOBA_BLOCK, c0:c0 + LANES], gk_ref[...])
            km_s[p, pl.ds(own, 1), :] = jnp.mean(kn, axis=0, keepdims=True)
            qt = head_norm(zq[r0:r0 + MOBA_BLOCK, c0:c0 + LANES], gq_ref[...]).T
            vt = zv[r0:r0 + MOBA_BLOCK, c0:c0 + LANES].T
            for hh, kh in enumerate((kn, pltpu.roll(kn, HEAD_DIM, 1))):
                ka_ref[0, 2 * p + hh, r0:r0 + MOBA_BLOCK, :] = jnp.where(low, kh, onehot).astype(BF16)
                vh = vt[hh * HEAD_DIM:(hh + 1) * HEAD_DIM]
                vt_ref[0, 2 * p + hh, sb] = jnp.concatenate([vh, ones_rows], axis=0).astype(BF16)
                hmask = (feat >= hh * HEAD_DIM) & (feat < (hh + 1) * HEAD_DIM)
                gate = jnp.dot(km_s[p], jnp.where(hmask, qt, 0.0), precision=lax.Precision.HIGHEST,
                               preferred_element_type=F32)
                g = jnp.where(past, gate, -jnp.inf)
                sel = blk == own
                for _ in range(MOBA_TOPK):
                    mx = jnp.max(g, axis=0, keepdims=True)
                    idx = jnp.min(jnp.where(g == mx, blk, nblk), axis=0, keepdims=True)
                    hit = blk == idx
                    sel = sel | (hit & past)
                    g = jnp.where(hit, -jnp.inf, g)
                qa_ref[0, 2 * p + hh, sb] = jnp.concatenate(
                    [qt[hh * HEAD_DIM:(hh + 1) * HEAD_DIM] * (QK_SCALE * LOG2E), jnp.where(sel, 0.0, NEG),
                     jnp.zeros((LANES - HEAD_DIM - nblk, MOBA_BLOCK), F32)], axis=0).astype(BF16)
    for task in rest:
        task()


def _inproj(l, x, mod, g, w, gq2, gk2):
    bsz, seq, _ = x.shape
    nb = seq // MOBA_BLOCK
    bpt = TM // MOBA_BLOCK
    tok = lambda b, t: (b, t, 0)
    return pl.pallas_call(
        _inproj_kernel,
        grid=(bsz, seq // TM),
        in_specs=[
            pl.BlockSpec((1, TM, D_MODEL), tok),
            _mod_spec(l),
            _layer_spec(l, (1, D_MODEL)),
            _layer_spec(l, (D_MODEL, IN_WIDTH)),
            _layer_spec(l, (1, LANES)),
            _layer_spec(l, (1, LANES)),
        ],
        out_specs=[
            pl.BlockSpec((1, TM, CONV_CH), tok),
            pl.BlockSpec((1, TM, 2 * D_MODEL), tok),
            pl.BlockSpec((1, N_HEADS, bpt, LANES, MOBA_BLOCK), lambda b, t: (b, 0, t, 0, 0)),
            pl.BlockSpec((1, N_HEADS, TM, LANES), lambda b, t: (b, 0, t, 0)),
            pl.BlockSpec((1, N_HEADS, bpt, V_ROWS, MOBA_BLOCK), lambda b, t: (b, 0, t, 0, 0)),
        ],
        out_shape=[
            jax.ShapeDtypeStruct((bsz, seq, CONV_CH), BF16),
            jax.ShapeDtypeStruct((bsz, seq, 2 * D_MODEL), BF16),
            jax.ShapeDtypeStruct((bsz, N_HEADS, nb, LANES, MOBA_BLOCK), BF16),
            jax.ShapeDtypeStruct((bsz, N_HEADS, seq, LANES), BF16),
            jax.ShapeDtypeStruct((bsz, N_HEADS, nb, V_ROWS, MOBA_BLOCK), BF16),
        ],
        scratch_shapes=[pltpu.VMEM((HEAD_PAIRS, nb, LANES), F32)],
        compiler_params=_params("arbitrary", "arbitrary"),
        name="norm_inproj",
    )(x, mod, g, w, gq2, gk2)


BIAS_STRIP = 32


def _bias_tile_kernel(rb_ref, o_ref):
    dblk = pl.program_id(0)
    max_exact = NUM_BUCKETS // 2
    for k0 in range(0, MOBA_BLOCK, BIAS_STRIP):
        k = k0 + lax.broadcasted_iota(jnp.int32, (BIAS_STRIP, MOBA_BLOCK), 0)
        q = lax.broadcasted_iota(jnp.int32, (BIAS_STRIP, MOBA_BLOCK), 1)
        d = dblk * MOBA_BLOCK + q - k
        dist = jnp.maximum(d, 0)
        nf = jnp.maximum(dist, 1).astype(F32)
        large = max_exact + (jnp.log(nf / max_exact) / math.log(MAX_DISTANCE / max_exact)
                             * (NUM_BUCKETS - max_exact)).astype(jnp.int32)
        large = jnp.minimum(large, NUM_BUCKETS - 1)
        bucket = jnp.where(dist < max_exact, dist, large)
        for h in range(N_HEADS):
            val = jnp.zeros((BIAS_STRIP, MOBA_BLOCK), F32)
            for b in range(NUM_BUCKETS):
                val = jnp.where(bucket == b, rb_ref[h, b], val)
            o_ref[h, 0, k0:k0 + BIAS_STRIP, :] = jnp.where(d >= 0, val * LOG2E, NEG)


def _bias_tiles(rel_bias):
    return pl.pallas_call(
        _bias_tile_kernel,
        grid=(N_BIAS_TILES,),
        in_specs=[pl.BlockSpec(memory_space=pltpu.SMEM)],
        out_specs=pl.BlockSpec((N_HEADS, 1, MOBA_BLOCK, MOBA_BLOCK), lambda d: (0, d, 0, 0)),
        out_shape=jax.ShapeDtypeStruct((N_HEADS, N_BIAS_TILES, MOBA_BLOCK, MOBA_BLOCK), F32),
        compiler_params=_params("arbitrary"),
        name="t5_bias_tiles",
    )(rel_bias)


def _key_groups(i):
    n_far = max(i - (FAR_BLOCKS - 1), 0)
    groups = []
    for lo, hi, near in ((0, n_far, False), (n_far, i + 1, True)):
        parts = -(-(hi - lo) // KB_GROUP)
        for k in range(parts):
            b0 = lo + (hi - lo) * k // parts
            b1 = lo + (hi - lo) * (k + 1) // parts
            groups.append((b0, b1 - b0, near))
    return groups


def _attn_kernel(qa_ref, ka_ref, vt_ref, bias_ref, o_ref, *, first_q):
    groups = [_key_groups(first_q + u) for u in range(Q_CLASS)]
    ahead = [None] * Q_CLASS
    carries = [None] * Q_CLASS
    for g in range(-1, max(len(gs) for gs in groups)):
        for u in range(Q_CLASS):
            if g >= len(groups[u]):
                continue
            q_augs = [qa_ref[0, hh, u] for hh in range(2)]
            scores, softmax_pv = _stream_ops(ka_ref, vt_ref, bias_ref, q_augs, first_q + u)
            if g < 0:
                carries[u] = tuple((jnp.full((1, MOBA_BLOCK), -jnp.inf, F32), jnp.zeros((V_ROWS, MOBA_BLOCK), F32))
                                   for _ in range(2))
                ahead[u] = scores(groups[u][0])
                continue
            ss = ahead[u]
            ahead[u] = scores(groups[u][g + 1]) if g + 1 < len(groups[u]) else None
            carries[u] = softmax_pv(ss, groups[u][g], carries[u])
    for u in range(Q_CLASS):
        outs = [acc[:HEAD_DIM] * (1.0 / acc[HEAD_DIM:HEAD_DIM + 1]) for _, acc in carries[u]]
        o_ref[0, u * MOBA_BLOCK:(u + 1) * MOBA_BLOCK, :] = jnp.concatenate(outs, axis=0).T.astype(BF16)


def _stream_ops(ka_ref, vt_ref, bias_ref, q_augs, i):
    def scores(group):
        b0, n, _ = group
        rows = [slice((b0 + u) * MOBA_BLOCK, (b0 + u + 1) * MOBA_BLOCK) for u in range(n)]
        return [[jnp.dot(ka_ref[0, hh, r, :], q_augs[hh], preferred_element_type=F32) for r in rows]
                for hh in range(2)]

    def softmax_pv(ss, group, carry):
        b0, n, near = group
        blocks = [b0 + u for u in range(n)]
        ps, stats = [], []
        for hh in range(2):
            m, _ = carry[hh]
            if near:
                s = [sv + bias_ref[hh, min(i - b, FAR_BLOCKS)] for sv, b in zip(ss[hh], blocks)]
                shift_by = 0.0
            else:
                s = ss[hh]
                shift_by = bias_ref[hh, FAR_BLOCKS, 0:1, 0:1]
            mx = jnp.max(functools.reduce(jnp.maximum, s), axis=0, keepdims=True)
            m_new = jnp.maximum(m, mx + shift_by)
            shift = m_new - shift_by
            alpha = jnp.exp2(m - m_new)
            stats.append((m_new, alpha))
            ps.append(jnp.concatenate([jnp.exp2(sv - shift).astype(BF16) for sv in s], axis=0))
        out = []
        for hh in range(2):
            m_new, alpha = stats[hh]
            vj = jnp.concatenate([vt_ref[0, hh, b] for b in blocks], axis=1)
            acc = alpha * carry[hh][1] + jnp.dot(vj, ps[hh], preferred_element_type=F32)
            out.append((m_new, acc))
        return tuple(out)

    return scores, softmax_pv


def _attn_instance(qa_ref, ka_ref, vt_ref, bias_ref, prev_ref, o_ref, *, first_q):
    del prev_ref
    _attn_kernel(qa_ref, ka_ref, vt_ref, bias_ref, o_ref, first_q=first_q)


def _attention(qa, ka, vt, bias):
    bsz, _, seq, _ = ka.shape
    nb = seq // MOBA_BLOCK
    out = jnp.zeros((bsz, seq, ATTN_WIDTH), BF16)
    for first in range(0, nb, Q_CLASS):
        keys = first + Q_CLASS
        out = pl.pallas_call(
            functools.partial(_attn_instance, first_q=first),
            grid=(bsz, HEAD_PAIRS),
            in_specs=[
                pl.BlockSpec((1, 2, Q_CLASS, LANES, MOBA_BLOCK),
                             lambda b, p, first=first: (b, p, first // Q_CLASS, 0, 0)),
                pl.BlockSpec((1, 2, keys * MOBA_BLOCK, LANES), lambda b, p: (b, p, 0, 0)),
                pl.BlockSpec((1, 2, keys, V_ROWS, MOBA_BLOCK), lambda b, p: (b, p, 0, 0, 0)),
                pl.BlockSpec((2, N_BIAS_TILES, MOBA_BLOCK, MOBA_BLOCK), lambda b, p: (p, 0, 0, 0)),
                pl.BlockSpec(memory_space=pl.ANY),
            ],
            out_specs=pl.BlockSpec((1, Q_CLASS * MOBA_BLOCK, LANES),
                                   lambda b, p, first=first: (b, first // Q_CLASS, p)),
            out_shape=jax.ShapeDtypeStruct((bsz, seq, ATTN_WIDTH), BF16),
            input_output_aliases={4: 0},
            compiler_params=_params("parallel", "arbitrary"),
            name=f"moba_attention_q{first}",
        )(qa, ka, vt, bias, out)
    return out


def _mixout_kernel(a_ref, halo_ref, zg_ref, o_ref, x_ref, mod_ref, cw_ref, cb_ref, lg_ref, lb_ref,
                   wpa_ref, wpb_ref, wo_ref, out_ref, a_ext, a_sh, y_buf):
    t = pl.program_id(1)
    a_ext[CONV_HALO:, :] = a_ref[0].astype(F32)
    a_ext[:CONV_HALO, :] = jnp.where(t > 0, halo_ref[0].astype(F32), 0.0)
    cw = cw_ref[...]
    cb = cb_ref[...]
    first = CONV_HALO - (CONV_KERNEL - 1)
    n_sh = a_sh.shape[1]
    for s in range(1, SUBLANES):
        a_sh[s - 1] = a_ext[s:s + n_sh, :]
    for r0 in range(0, TM, CONV_ROWS):
        acc = jnp.broadcast_to(cb, (CONV_ROWS, CONV_CH))
        for j in range(CONV_KERNEL):
            q, s = divmod(first + j, SUBLANES)
            lo = r0 + SUBLANES * q
            tap = a_ext[lo:lo + CONV_ROWS, :] if s == 0 else a_sh[s - 1, lo:lo + CONV_ROWS, :]
            acc = acc + cw[j:j + 1] * tap
        y_buf[r0:r0 + CONV_ROWS, :] = acc
    y = y_buf[...]
    mu = jnp.mean(y, axis=-1, keepdims=True)
    yc = y - mu
    var = jnp.mean(yc * yc, axis=-1, keepdims=True)
    yn = yc * lax.rsqrt(var + EPS) * lg_ref[...] + lb_ref[...]
    act = (yn * _sigmoid(yn)).astype(BF16)
    ya = jnp.dot(act, wpa_ref[...], preferred_element_type=F32)
    yb = jnp.dot(o_ref[0], wpb_ref[...], preferred_element_type=F32)
    ga = zg_ref[0, :, :D_MODEL].astype(F32)
    gb = zg_ref[0, :, D_MODEL:].astype(F32)
    m = (_sigmoid(ga) * ya + _sigmoid(gb) * yb).astype(BF16)
    y = jnp.dot(m, wo_ref[...], preferred_element_type=F32)
    out_ref[0] = x_ref[0] + mod_ref[0][2:3] * y


def _mixout(l, a, zg, o, x, mod, cw, cb, lg, lb, wpa, wpb, wo):
    bsz, seq, _ = x.shape
    per = TM // CONV_HALO
    tok = lambda b, t: (b, t, 0)
    return pl.pallas_call(
        _mixout_kernel,
        grid=(bsz, seq // TM),
        in_specs=[
            pl.BlockSpec((1, TM, CONV_CH), tok),
            pl.BlockSpec((1, CONV_HALO, CONV_CH), lambda b, t: (b, jnp.maximum(t * per - 1, 0), 0)),
            pl.BlockSpec((1, TM, 2 * D_MODEL), tok),
            pl.BlockSpec((1, TM, ATTN_WIDTH), tok),
            pl.BlockSpec((1, TM, D_MODEL), tok),
            _mod_spec(l),
            _layer_spec(l, (CONV_HALO, CONV_CH)),
            _layer_spec(l, (1, CONV_CH)),
            _layer_spec(l, (1, CONV_CH)),
            _layer_spec(l, (1, CONV_CH)),
            _layer_spec(l, (CONV_CH, D_MODEL)),
            _layer_spec(l, (ATTN_WIDTH, D_MODEL)),
            _layer_spec(l, (D_MODEL, D_MODEL)),
        ],
        out_specs=pl.BlockSpec((1, TM, D_MODEL), tok),
        out_shape=jax.ShapeDtypeStruct((bsz, seq, D_MODEL), F32),
        scratch_shapes=[pltpu.VMEM((TM + CONV_HALO, CONV_CH), F32),
                        pltpu.VMEM((SUBLANES - 1, TM + CONV_HALO - SUBLANES, CONV_CH), F32),
                        pltpu.VMEM((TM, CONV_CH), F32)],
        compiler_params=_params("parallel", "arbitrary"),
        name="mixer_out",
    )(a, a, zg, o, x, mod, cw, cb, lg, lb, wpa, wpb, wo)


def _ffn_kernel(x_ref, halo_ref, mod_ref, g_ref, wup_ref, fcw_ref, fcb_ref, wdn_ref, out_ref, act_buf):
    t = pl.program_id(1)
    mod = mod_ref[0]
    x = x_ref[0]
    xe = jnp.concatenate([halo_ref[0], x], axis=0)
    h = _mod_rmsnorm(xe, g_ref[...], mod[3:4], mod[4:5])
    row = lax.broadcasted_iota(jnp.int32, (FFN_HALO + TM, 1), 0)
    h = jnp.where((row >= FFN_HALO) | (t > 0), h, 0.0).astype(BF16)

    def conv(col):
        u = jnp.dot(h, wup_ref[:, col:col + FF_CHUNK], preferred_element_type=F32)
        w = fcw_ref[:, col:col + FF_CHUNK]
        y = w[2:3] * u + w[1:2] * pltpu.roll(u, 1, 0) + w[0:1] * pltpu.roll(u, 2, 0)
        return y[FFN_HALO:] + fcb_ref[:, col:col + FF_CHUNK]

    def down(g0, g1):
        return jnp.dot(act_buf[:, g0:g1], wdn_ref[g0:g1, :], preferred_element_type=F32)

    acc = None
    pending = None
    assert sum(FF_GROUPS) * FF_CHUNK == D_FF
    bounds = [FF_CHUNK * sum(FF_GROUPS[:k]) for k in range(len(FF_GROUPS) + 1)]
    for g0, g1 in zip(bounds[:-1], bounds[1:]):
        for c0 in range(g0, g1, FF_CHUNK):
            uv = conv(c0)
            ug = conv(D_FF + c0)
            act_buf[:, c0:c0 + FF_CHUNK] = (ug * _sigmoid(ug) * uv).astype(BF16)
            if pending is not None:
                part = down(*pending)
                acc = part if acc is None else acc + part
                pending = None
        pending = (g0, g1)
    part = down(*pending)
    acc = part if acc is None else acc + part
    out_ref[0] = x + mod[5:6] * acc


def _ffn(l, x, mod, g, wup, fcw, fcb, wdn):
    bsz, seq, _ = x.shape
    per = TM // FFN_HALO
    return pl.pallas_call(
        _ffn_kernel,
        grid=(bsz, seq // TM),
        in_specs=[
            pl.BlockSpec((1, TM, D_MODEL), lambda b, t: (b, t, 0)),
            pl.BlockSpec((1, FFN_HALO, D_MODEL), lambda b, t: (b, jnp.maximum(t * per - 1, 0), 0)),
            _mod_spec(l),
            _layer_spec(l, (1, D_MODEL)),
            _layer_spec(l, (D_MODEL, 2 * D_FF)),
            _layer_spec(l, (SUBLANES, 2 * D_FF)),
            _layer_spec(l, (1, 2 * D_FF)),
            _layer_spec(l, (D_FF, D_MODEL)),
        ],
        out_specs=pl.BlockSpec((1, TM, D_MODEL), lambda b, t: (b, t, 0)),
        out_shape=jax.ShapeDtypeStruct((bsz, seq, D_MODEL), F32),
        scratch_shapes=[pltpu.VMEM((TM, D_FF), BF16)],
        compiler_params=_params("parallel", "arbitrary"),
        name="conv_ffn",
    )(x, x, mod, g, wup, fcw, fcb, wdn)


def kernel(x, c, ada_w, ada_b, norm1_g, w_in, conv_w, conv_b, conv_ln_g, conv_ln_b, q_norm_g, k_norm_g,
           w_pa, w_pb, w_o, norm2_g, w_up, ffn_conv_w, ffn_conv_b, w_down, rel_bias):
    mod = _modulation(c, ada_w, ada_b)
    bias = _bias_tiles(rel_bias)
    row = lambda p: p[:, None, :]
    cw = jnp.pad(conv_w, ((0, 0), (0, CONV_HALO - CONV_KERNEL), (0, 0)))
    fcw = jnp.pad(ffn_conv_w, ((0, 0), (0, SUBLANES - FFN_CONV), (0, 0)))
    gq2, gk2 = row(jnp.tile(q_norm_g, (1, 2))), row(jnp.tile(k_norm_g, (1, 2)))
    w_in, w_pa, w_pb, w_o, w_up, w_down = (w.astype(BF16) for w in (w_in, w_pa, w_pb, w_o, w_up, w_down))
    for l in range(DEPTH):
        a, zg, qa, ka, vt = _inproj(l, x, mod, row(norm1_g), w_in, gq2, gk2)
        o = _attention(qa, ka, vt, bias)
        x = _mixout(l, a, zg, o, x, mod, cw, row(conv_b), row(conv_ln_g), row(conv_ln_b), w_pa, w_pb, w_o)
        x = _ffn(l, x, mod, row(norm2_g), w_up, fcw, row(ffn_conv_b), w_down)
    return x
```

```python
import functools
import math

import jax
import jax.numpy as jnp
from jax import lax
from jax.experimental import pallas as pl
from jax.experimental.pallas import tpu as pltpu

F32 = jnp.float32
BF16 = jnp.bfloat16

D_MODEL = 1024
DEPTH = 4
CONV_CH = 512
CONV_KERNEL = 31
HEAD_DIM = 64
N_HEADS = 8
ATTN_WIDTH = 512
MOBA_BLOCK = 256
MOBA_TOPK = 3
NUM_BUCKETS = 32
MAX_DISTANCE = 2048
D_FF = 2816
FFN_CONV = 3
EPS = 1e-6
IN_WIDTH = 2 * CONV_CH + 3 * ATTN_WIDTH + 2 * D_MODEL
Q_COL = 2 * CONV_CH
K_COL = Q_COL + ATTN_WIDTH
V_COL = K_COL + ATTN_WIDTH
GATE_COL = V_COL + ATTN_WIDTH

LANES = 128
SUBLANES = 8
HEAD_PAIRS = N_HEADS // 2
LOG2E = math.log2(math.e)
QK_SCALE = HEAD_DIM ** -0.5
NEG = -1e30
V_ROWS = HEAD_DIM + 16
FAR_BLOCKS = -(-(MAX_DISTANCE + MOBA_BLOCK) // MOBA_BLOCK)
N_BIAS_TILES = FAR_BLOCKS + 1
CONV_HALO = 32
CONV_ROWS = 32
FFN_HALO = SUBLANES
VMEM_LIMIT = 56 * 1024 * 1024

TM = 512
FF_CHUNK = 256
FF_GROUPS = (4, 4, 2, 1)
KB_GROUP = 6
Q_CLASS = 4


def _sigmoid(x):
    return 1.0 / (1.0 + jnp.exp(-x))


def _params(*sem):
    return pltpu.CompilerParams(dimension_semantics=sem, vmem_limit_bytes=VMEM_LIMIT)


def _layer_spec(l, shape):
    nd = len(shape)
    return pl.BlockSpec((None,) + shape, lambda *_: (l,) + (0,) * nd, pipeline_mode=pl.Buffered(1))


def _mod_spec(l):
    return pl.BlockSpec((None, 1, SUBLANES, D_MODEL), lambda b, t: (l, b, 0, 0))


def _mod_kernel(c_ref, w_ref, b_ref, o_ref):
    c = c_ref[...]
    cond = c * _sigmoid(c)
    o_ref[0] = jnp.dot(cond, w_ref[0], precision=lax.Precision.HIGHEST,
                       preferred_element_type=F32) + b_ref[0]


def _modulation(c, ada_w, ada_b):
    bsz = c.shape[0]
    c8 = jnp.pad(c, ((0, SUBLANES - bsz), (0, 0)))
    tn = 1536
    n = 6 * D_MODEL
    out = pl.pallas_call(
        _mod_kernel,
        grid=(DEPTH, n // tn),
        in_specs=[
            pl.BlockSpec((SUBLANES, D_MODEL), lambda l, j: (0, 0)),
            pl.BlockSpec((1, D_MODEL, tn), lambda l, j: (l, 0, j)),
            pl.BlockSpec((1, 1, tn), lambda l, j: (l, 0, j)),
        ],
        out_specs=pl.BlockSpec((1, SUBLANES, tn), lambda l, j: (l, 0, j)),
        out_shape=jax.ShapeDtypeStruct((DEPTH, SUBLANES, n), F32),
        compiler_params=_params("arbitrary", "arbitrary"),
        name="adaln_modulation",
    )(c8, ada_w, ada_b.reshape(DEPTH, 1, n))
    mod = out[:, :bsz].reshape(DEPTH, bsz, 6, D_MODEL)
    return jnp.pad(mod, ((0, 0), (0, 0), (0, SUBLANES - 6), (0, 0)))


def _mod_rmsnorm(x, g, shift, scale):
    ms = jnp.mean(x * x, axis=-1, keepdims=True)
    return (x * lax.rsqrt(ms + EPS) * g) * (1.0 + scale) + shift


def _inproj_kernel(x_ref, mod_ref, g_ref, w_ref, gq_ref, gk_ref, a_ref, zg_ref, qa_ref, ka_ref, vt_ref, km_s):
    t = pl.program_id(1)

    @pl.when(t == 0)
    def _():
        km_s[...] = jnp.zeros(km_s.shape, F32)

    mod = mod_ref[0]
    h = _mod_rmsnorm(x_ref[0], g_ref[...], mod[0:1], mod[1:2]).astype(BF16)
    tn = 512

    def proj(col):
        return jnp.dot(h, w_ref[:, col:col + tn], preferred_element_type=F32)

    zq, zk, zv = proj(Q_COL), proj(K_COL), proj(V_COL)

    def glu():
        a_ref[0] = (proj(0) * _sigmoid(proj(CONV_CH))).astype(BF16)

    def gate_cols(n0):
        zg_ref[0, :, n0:n0 + tn] = proj(GATE_COL + n0).astype(BF16)

    rest = [glu] + [functools.partial(gate_cols, n0) for n0 in range(0, 2 * D_MODEL, tn)]
    lane = lax.broadcasted_iota(jnp.int32, (MOBA_BLOCK, LANES), 1)
    low = lane < HEAD_DIM

    def head_norm(tv, g):
        t2 = tv * tv
        s0 = jnp.sum(jnp.where(low, t2, 0.0), axis=-1, keepdims=True)
        s1 = jnp.sum(jnp.where(low, 0.0, t2), axis=-1, keepdims=True)
        ms = jnp.where(low, s0, s1) * (1.0 / HEAD_DIM)
        return tv * lax.rsqrt(ms + EPS) * g

    blocks_per_tile = TM // MOBA_BLOCK
    nblk = km_s.shape[1]
    ones_rows = jnp.where(lax.broadcasted_iota(jnp.int32, (V_ROWS - HEAD_DIM, MOBA_BLOCK), 0) == 0, 1.0, 0.0)
    feat = lax.broadcasted_iota(jnp.int32, (LANES, MOBA_BLOCK), 0)
    blk = lax.broadcasted_iota(jnp.int32, (nblk, MOBA_BLOCK), 0)
    for sb in range(blocks_per_tile):
        r0 = sb * MOBA_BLOCK
        own = t * blocks_per_tile + sb
        onehot = jnp.where(lane - HEAD_DIM == own, 1.0, 0.0)
        past = blk < own
        for p in range(HEAD_PAIRS):
            if rest:
                rest.pop(0)()
            c0 = p * LANES
            kn = head_norm(zk[r0:r0 + MOBA_BLOCK, c0:c0 + LANES], gk_ref[...])
            km_s[p, pl.ds(own, 1), :] = jnp.mean(kn, axis=0, keepdims=True)
            qt = head_norm(zq[r0:r0 + MOBA_BLOCK, c0:c0 + LANES], gq_ref[...]).T
            vt = zv[r0:r0 + MOBA_BLOCK, c0:c0 + LANES].T
            for hh, kh in enumerate((kn, pltpu.roll(kn, HEAD_DIM, 1))):
                ka_ref[0, 2 * p + hh, r0:r0 + MOBA_BLOCK, :] = jnp.where(low, kh, onehot).astype(BF16)
                vh = vt[hh * HEAD_DIM:(hh + 1) * HEAD_DIM]
                vt_ref[0, 2 * p + hh, sb] = jnp.concatenate([vh, ones_rows], axis=0).astype(BF16)
                hmask = (feat >= hh * HEAD_DIM) & (feat < (hh + 1) * HEAD_DIM)
                gate = jnp.dot(km_s[p], jnp.where(hmask, qt, 0.0), precision=lax.Precision.HIGHEST,
                               preferred_element_type=F32)
                g = jnp.where(past, gate, -jnp.inf)
                sel = blk == own
                for _ in range(MOBA_TOPK):
                    mx = jnp.max(g, axis=0, keepdims=True)
                    idx = jnp.min(jnp.where(g == mx, blk, nblk), axis=0, keepdims=True)
                    hit = blk == idx
                    sel = sel | (hit & past)
                    g = jnp.where(hit, -jnp.inf, g)
                qa_ref[0, 2 * p + hh, sb] = jnp.concatenate(
                    [qt[hh * HEAD_DIM:(hh + 1) * HEAD_DIM] * (QK_SCALE * LOG2E), jnp.where(sel, 0.0, NEG),
                     jnp.zeros((LANES - HEAD_DIM - nblk, MOBA_BLOCK), F32)], axis=0).astype(BF16)
    for task in rest:
        task()


def _inproj(l, x, mod, g, w, gq2, gk2):
    bsz, seq, _ = x.shape
    nb = seq // MOBA_BLOCK
    bpt = TM // MOBA_BLOCK
    tok = lambda b, t: (b, t, 0)
    return pl.pallas_call(
        _inproj_kernel,
        grid=(bsz, seq // TM),
        in_specs=[
            pl.BlockSpec((1, TM, D_MODEL), tok),
            _mod_spec(l),
            _layer_spec(l, (1, D_MODEL)),
            _layer_spec(l, (D_MODEL, IN_WIDTH)),
            _layer_spec(l, (1, LANES)),
            _layer_spec(l, (1, LANES)),
        ],
        out_specs=[
            pl.BlockSpec((1, TM, CONV_CH), tok),
            pl.BlockSpec((1, TM, 2 * D_MODEL), tok),
            pl.BlockSpec((1, N_HEADS, bpt, LANES, MOBA_BLOCK), lambda b, t: (b, 0, t, 0, 0)),
            pl.BlockSpec((1, N_HEADS, TM, LANES), lambda b, t: (b, 0, t, 0)),
            pl.BlockSpec((1, N_HEADS, bpt, V_ROWS, MOBA_BLOCK), lambda b, t: (b, 0, t, 0, 0)),
        ],
        out_shape=[
            jax.ShapeDtypeStruct((bsz, seq, CONV_CH), BF16),
            jax.ShapeDtypeStruct((bsz, seq, 2 * D_MODEL), BF16),
            jax.ShapeDtypeStruct((bsz, N_HEADS, nb, LANES, MOBA_BLOCK), BF16),
            jax.ShapeDtypeStruct((bsz, N_HEADS, seq, LANES), BF16),
            jax.ShapeDtypeStruct((bsz, N_HEADS, nb, V_ROWS, MOBA_BLOCK), BF16),
        ],
        scratch_shapes=[pltpu.VMEM((HEAD_PAIRS, nb, LANES), F32)],
        compiler_params=_params("arbitrary", "arbitrary"),
        name="norm_inproj",
    )(x, mod, g, w, gq2, gk2)


BIAS_STRIP = 32


def _bias_tile_kernel(rb_ref, o_ref):
    dblk = pl.program_id(0)
    max_exact = NUM_BUCKETS // 2
    for k0 in range(0, MOBA_BLOCK, BIAS_STRIP):
        k = k0 + lax.broadcasted_iota(jnp.int32, (BIAS_STRIP, MOBA_BLOCK), 0)
        q = lax.broadcasted_iota(jnp.int32, (BIAS_STRIP, MOBA_BLOCK), 1)
        d = dblk * MOBA_BLOCK + q - k
        dist = jnp.maximum(d, 0)
        nf = jnp.maximum(dist, 1).astype(F32)
        large = max_exact + (jnp.log(nf / max_exact) / math.log(MAX_DISTANCE / max_exact)
                             * (NUM_BUCKETS - max_exact)).astype(jnp.int32)
        large = jnp.minimum(large, NUM_BUCKETS - 1)
        bucket = jnp.where(dist < max_exact, dist, large)
        for h in range(N_HEADS):
            val = jnp.zeros((BIAS_STRIP, MOBA_BLOCK), F32)
            for b in range(NUM_BUCKETS):
                val = jnp.where(bucket == b, rb_ref[h, b], val)
            o_ref[h, 0, k0:k0 + BIAS_STRIP, :] = jnp.where(d >= 0, val * LOG2E, NEG)


def _bias_tiles(rel_bias):
    return pl.pallas_call(
        _bias_tile_kernel,
        grid=(N_BIAS_TILES,),
        in_specs=[pl.BlockSpec(memory_space=pltpu.SMEM)],
        out_specs=pl.BlockSpec((N_HEADS, 1, MOBA_BLOCK, MOBA_BLOCK), lambda d: (0, d, 0, 0)),
        out_shape=jax.ShapeDtypeStruct((N_HEADS, N_BIAS_TILES, MOBA_BLOCK, MOBA_BLOCK), F32),
        compiler_params=_params("arbitrary"),
        name="t5_bias_tiles",
    )(rel_bias)


def _key_groups(i):
    n_far = max(i - (FAR_BLOCKS - 1), 0)
    groups = []
    for lo, hi, near in ((0, n_far, False), (n_far, i + 1, True)):
        parts = -(-(hi - lo) // KB_GROUP)
        for k in range(parts):
            b0 = lo + (hi - lo) * k // parts
            b1 = lo + (hi - lo) * (k + 1) // parts
            groups.append((b0, b1 - b0, near))
    return groups


def _attn_kernel(qa_ref, ka_ref, vt_ref, bias_ref, o_ref, *, first_q):
    groups = [_key_groups(first_q + u) for u in range(Q_CLASS)]
    ahead = [None] * Q_CLASS
    carries = [None] * Q_CLASS
    for g in range(-1, max(len(gs) for gs in groups)):
        for u in range(Q_CLASS):
            if g >= len(groups[u]):
                continue
            q_augs = [qa_ref[0, hh, u] for hh in range(2)]
            scores, softmax_pv = _stream_ops(ka_ref, vt_ref, bias_ref, q_augs, first_q + u)
            if g < 0:
                carries[u] = tuple((jnp.full((1, MOBA_BLOCK), -jnp.inf, F32), jnp.zeros((V_ROWS, MOBA_BLOCK), F32))
                                   for _ in range(2))
                ahead[u] = scores(groups[u][0])
                continue
            ss = ahead[u]
            ahead[u] = scores(groups[u][g + 1]) if g + 1 < len(groups[u]) else None
            carries[u] = softmax_pv(ss, groups[u][g], carries[u])
    for u in range(Q_CLASS):
        outs = [acc[:HEAD_DIM] * (1.0 / acc[HEAD_DIM:HEAD_DIM + 1]) for _, acc in carries[u]]
        o_ref[0, u * MOBA_BLOCK:(u + 1) * MOBA_BLOCK, :] = jnp.concatenate(outs, axis=0).T.astype(BF16)


def _stream_ops(ka_ref, vt_ref, bias_ref, q_augs, i):
    def scores(group):
        b0, n, _ = group
        rows = [slice((b0 + u) * MOBA_BLOCK, (b0 + u + 1) * MOBA_BLOCK) for u in range(n)]
        return [[jnp.dot(ka_ref[0, hh, r, :], q_augs[hh], preferred_element_type=F32) for r in rows]
                for hh in range(2)]

    def softmax_pv(ss, group, carry):
        b0, n, near = group
        blocks = [b0 + u for u in range(n)]
        ps, stats = [], []
        for hh in range(2):
            m, _ = carry[hh]
            if near:
                s = [sv + bias_ref[hh, min(i - b, FAR_BLOCKS)] for sv, b in zip(ss[hh], blocks)]
                shift_by = 0.0
            else:
                s = ss[hh]
                shift_by = bias_ref[hh, FAR_BLOCKS, 0:1, 0:1]
            mx = jnp.max(functools.reduce(jnp.maximum, s), axis=0, keepdims=True)
            m_new = jnp.maximum(m, mx + shift_by)
            shift = m_new - shift_by
            alpha = jnp.exp2(m - m_new)
            stats.append((m_new, alpha))
            ps.append(jnp.concatenate([jnp.exp2(sv - shift).astype(BF16) for sv in s], axis=0))
        out = []
        for hh in range(2):
            m_new, alpha = stats[hh]
            vj = jnp.concatenate([vt_ref[0, hh, b] for b in blocks], axis=1)
            acc = alpha * carry[hh][1] + jnp.dot(vj, ps[hh], preferred_element_type=F32)
            out.append((m_new, acc))
        return tuple(out)

    return scores, softmax_pv


def _attn_instance(qa_ref, ka_ref, vt_ref, bias_ref, prev_ref, o_ref, *, first_q):
    del prev_ref
    _attn_kernel(qa_ref, ka_ref, vt_ref, bias_ref, o_ref, first_q=first_q)


def _attention(qa, ka, vt, bias):
    bsz, _, seq, _ = ka.shape
    nb = seq // MOBA_BLOCK
    out = jnp.zeros((bsz, seq, ATTN_WIDTH), BF16)
    for first in range(0, nb, Q_CLASS):
        keys = first + Q_CLASS
        out = pl.pallas_call(
            functools.partial(_attn_instance, first_q=first),
            grid=(bsz, HEAD_PAIRS),
            in_specs=[
                pl.BlockSpec((1, 2, Q_CLASS, LANES, MOBA_BLOCK),
                             lambda b, p, first=first: (b, p, first // Q_CLASS, 0, 0)),
                pl.BlockSpec((1, 2, keys * MOBA_BLOCK, LANES), lambda b, p: (b, p, 0, 0)),
                pl.BlockSpec((1, 2, keys, V_ROWS, MOBA_BLOCK), lambda b, p: (b, p, 0, 0, 0)),
                pl.BlockSpec((2, N_BIAS_TILES, MOBA_BLOCK, MOBA_BLOCK), lambda b, p: (p, 0, 0, 0)),
                pl.BlockSpec(memory_space=pl.ANY),
            ],
            out_specs=pl.BlockSpec((1, Q_CLASS * MOBA_BLOCK, LANES),
                                   lambda b, p, first=first: (b, first // Q_CLASS, p)),
            out_shape=jax.ShapeDtypeStruct((bsz, seq, ATTN_WIDTH), BF16),
            input_output_aliases={4: 0},
            compiler_params=_params("parallel", "arbitrary"),
            name=f"moba_attention_q{first}",
        )(qa, ka, vt, bias, out)
    return out


def _mixout_kernel(a_ref, halo_ref, zg_ref, o_ref, x_ref, mod_ref, cw_ref, cb_ref, lg_ref, lb_ref,
                   wpa_ref, wpb_ref, wo_ref, out_ref, a_ext, a_sh, y_buf):
    t = pl.program_id(1)
    a_ext[CONV_HALO:, :] = a_ref[0].astype(F32)
    a_ext[:CONV_HALO, :] = jnp.where(t > 0, halo_ref[0].astype(F32), 0.0)
    cw = cw_ref[...]
    cb = cb_ref[...]
    first = CONV_HALO - (CONV_KERNEL - 1)
    n_sh = a_sh.shape[1]
    for s in range(1, SUBLANES):
        a_sh[s - 1] = a_ext[s:s + n_sh, :]
    for r0 in range(0, TM, CONV_ROWS):
        acc = jnp.broadcast_to(cb, (CONV_ROWS, CONV_CH))
        for j in range(CONV_KERNEL):
            q, s = divmod(first + j, SUBLANES)
            lo = r0 + SUBLANES * q
            tap = a_ext[lo:lo + CONV_ROWS, :] if s == 0 else a_sh[s - 1, lo:lo + CONV_ROWS, :]
            acc = acc + cw[j:j + 1] * tap
        y_buf[r0:r0 + CONV_ROWS, :] = acc
    y = y_buf[...]
    mu = jnp.mean(y, axis=-1, keepdims=True)
    yc = y - mu
    var = jnp.mean(yc * yc, axis=-1, keepdims=True)
    yn = yc * lax.rsqrt(var + EPS) * lg_ref[...] + lb_ref[...]
    act = (yn * _sigmoid(yn)).astype(BF16)
    ya = jnp.dot(act, wpa_ref[...], preferred_element_type=F32)
    yb = jnp.dot(o_ref[0], wpb_ref[...], preferred_element_type=F32)
    ga = zg_ref[0, :, :D_MODEL].astype(F32)
    gb = zg_ref[0, :, D_MODEL:].astype(F32)
    m = (_sigmoid(ga) * ya + _sigmoid(gb) * yb).astype(BF16)
    y = jnp.dot(m, wo_ref[...], preferred_element_type=F32)
    out_ref[0] = x_ref[0] + mod_ref[0][2:3] * y


def _mixout(l, a, zg, o, x, mod, cw, cb, lg, lb, wpa, wpb, wo):
    bsz, seq, _ = x.shape
    per = TM // CONV_HALO
    tok = lambda b, t: (b, t, 0)
    return pl.pallas_call(
        _mixout_kernel,
        grid=(bsz, seq // TM),
        in_specs=[
            pl.BlockSpec((1, TM, CONV_CH), tok),
            pl.BlockSpec((1, CONV_HALO, CONV_CH), lambda b, t: (b, jnp.maximum(t * per - 1, 0), 0)),
            pl.BlockSpec((1, TM, 2 * D_MODEL), tok),
            pl.BlockSpec((1, TM, ATTN_WIDTH), tok),
            pl.BlockSpec((1, TM, D_MODEL), tok),
            _mod_spec(l),
            _layer_spec(l, (CONV_HALO, CONV_CH)),
            _layer_spec(l, (1, CONV_CH)),
            _layer_spec(l, (1, CONV_CH)),
            _layer_spec(l, (1, CONV_CH)),
            _layer_spec(l, (CONV_CH, D_MODEL)),
            _layer_spec(l, (ATTN_WIDTH, D_MODEL)),
            _layer_spec(l, (D_MODEL, D_MODEL)),
        ],
        out_specs=pl.BlockSpec((1, TM, D_MODEL), tok),
        out_shape=jax.ShapeDtypeStruct((bsz, seq, D_MODEL), F32),
        scratch_shapes=[pltpu.VMEM((TM + CONV_HALO, CONV_CH), F32),
                        pltpu.VMEM((SUBLANES - 1, TM + CONV_HALO - SUBLANES, CONV_CH), F32),
                        pltpu.VMEM((TM, CONV_CH), F32)],
        compiler_params=_params("parallel", "arbitrary"),
        name="mixer_out",
    )(a, a, zg, o, x, mod, cw, cb, lg, lb, wpa, wpb, wo)


def _ffn_kernel(x_ref, mod_ref, g_ref, wup_ref, fcw_ref, fcb_ref, wdn_ref, out_ref, act_buf, u_tail):
    @pl.when(pl.program_id(1) == 0)
    def _():
        u_tail[...] = jnp.zeros(u_tail.shape, F32)

    mod = mod_ref[0]
    x = x_ref[0]
    h = _mod_rmsnorm(x, g_ref[...], mod[3:4], mod[4:5]).astype(BF16)

    def conv(col):
        u = jnp.dot(h, wup_ref[:, col:col + FF_CHUNK], preferred_element_type=F32)
        ue = jnp.concatenate([u_tail[:, col:col + FF_CHUNK], u], axis=0)
        u_tail[:, col:col + FF_CHUNK] = u[TM - FFN_HALO:]
        w = fcw_ref[:, col:col + FF_CHUNK]
        y = w[2:3] * ue + w[1:2] * pltpu.roll(ue, 1, 0) + w[0:1] * pltpu.roll(ue, 2, 0)
        return y[FFN_HALO:] + fcb_ref[:, col:col + FF_CHUNK]

    def down(g0, g1):
        return jnp.dot(act_buf[:, g0:g1], wdn_ref[g0:g1, :], preferred_element_type=F32)

    acc = None
    pending = None
    assert sum(FF_GROUPS) * FF_CHUNK == D_FF
    bounds = [FF_CHUNK * sum(FF_GROUPS[:k]) for k in range(len(FF_GROUPS) + 1)]
    for g0, g1 in zip(bounds[:-1], bounds[1:]):
        for c0 in range(g0, g1, FF_CHUNK):
            uv = conv(c0)
            ug = conv(D_FF + c0)
            act_buf[:, c0:c0 + FF_CHUNK] = (ug * _sigmoid(ug) * uv).astype(BF16)
            if pending is not None:
                part = down(*pending)
                acc = part if acc is None else acc + part
                pending = None
        pending = (g0, g1)
    part = down(*pending)
    acc = part if acc is None else acc + part
    out_ref[0] = x + mod[5:6] * acc


def _ffn(l, x, mod, g, wup, fcw, fcb, wdn):
    bsz, seq, _ = x.shape
    return pl.pallas_call(
        _ffn_kernel,
        grid=(bsz, seq // TM),
        in_specs=[
            pl.BlockSpec((1, TM, D_MODEL), lambda b, t: (b, t, 0)),
            _mod_spec(l),
            _layer_spec(l, (1, D_MODEL)),
            _layer_spec(l, (D_MODEL, 2 * D_FF)),
            _layer_spec(l, (SUBLANES, 2 * D_FF)),
            _layer_spec(l, (1, 2 * D_FF)),
            _layer_spec(l, (D_FF, D_MODEL)),
        ],
        out_specs=pl.BlockSpec((1, TM, D_MODEL), lambda b, t: (b, t, 0)),
        out_shape=jax.ShapeDtypeStruct((bsz, seq, D_MODEL), F32),
        scratch_shapes=[pltpu.VMEM((TM, D_FF), BF16), pltpu.VMEM((FFN_HALO, 2 * D_FF), F32)],
        compiler_params=_params("arbitrary", "arbitrary"),
        name="conv_ffn",
    )(x, mod, g, wup, fcw, fcb, wdn)


def kernel(x, c, ada_w, ada_b, norm1_g, w_in, conv_w, conv_b, conv_ln_g, conv_ln_b, q_norm_g, k_norm_g,
           w_pa, w_pb, w_o, norm2_g, w_up, ffn_conv_w, ffn_conv_b, w_down, rel_bias):
    mod = _modulation(c, ada_w, ada_b)
    bias = _bias_tiles(rel_bias)
    row = lambda p: p[:, None, :]
    cw = jnp.pad(conv_w, ((0, 0), (0, CONV_HALO - CONV_KERNEL), (0, 0)))
    fcw = jnp.pad(ffn_conv_w, ((0, 0), (0, SUBLANES - FFN_CONV), (0, 0)))
    gq2, gk2 = row(jnp.tile(q_norm_g, (1, 2))), row(jnp.tile(k_norm_g, (1, 2)))
    w_in, w_pa, w_pb, w_o, w_up, w_down = (w.astype(BF16) for w in (w_in, w_pa, w_pb, w_o, w_up, w_down))
    for l in range(DEPTH):
        a, zg, qa, ka, vt = _inproj(l, x, mod, row(norm1_g), w_in, gq2, gk2)
        o = _attention(qa, ka, vt, bias)
        x = _mixout(l, a, zg, o, x, mod, cw, row(conv_b), row(conv_ln_g), row(conv_ln_b), w_pa, w_pb, w_o)
        x = _ffn(l, x, mod, row(norm2_g), w_up, fcw, row(ffn_conv_b), w_down)
    return x
```

```python
import functools
import math

import jax
import jax.numpy as jnp
from jax import lax
from jax.experimental import pallas as pl
from jax.experimental.pallas import tpu as pltpu

F32 = jnp.float32
BF16 = jnp.bfloat16

D_MODEL = 1024
DEPTH = 4
CONV_CH = 512
CONV_KERNEL = 31
HEAD_DIM = 64
N_HEADS = 8
ATTN_WIDTH = 512
MOBA_BLOCK = 256
MOBA_TOPK = 3
NUM_BUCKETS = 32
MAX_DISTANCE = 2048
D_FF = 2816
FFN_CONV = 3
EPS = 1e-6
IN_WIDTH = 2 * CONV_CH + 3 * ATTN_WIDTH + 2 * D_MODEL
Q_COL = 2 * CONV_CH
K_COL = Q_COL + ATTN_WIDTH
V_COL = K_COL + ATTN_WIDTH
GATE_COL = V_COL + ATTN_WIDTH

LANES = 128
SUBLANES = 8
HEAD_PAIRS = N_HEADS // 2
LOG2E = math.log2(math.e)
QK_SCALE = HEAD_DIM ** -0.5
NEG = -1e30
V_ROWS = HEAD_DIM + 16
FAR_BLOCKS = -(-(MAX_DISTANCE + MOBA_BLOCK) // MOBA_BLOCK)
N_BIAS_TILES = FAR_BLOCKS + 1
CONV_HALO = 32
CONV_ROWS = 32
FFN_HALO = SUBLANES
VMEM_LIMIT = 56 * 1024 * 1024

TM = 512
FF_CHUNK = 256
FF_GROUPS = (4, 4, 2, 1)
KB_GROUP = 6
Q_CLASS = 4


def _sigmoid(x):
    return 1.0 / (1.0 + jnp.exp(-x))


def _params(*sem):
    return pltpu.CompilerParams(dimension_semantics=sem, vmem_limit_bytes=VMEM_LIMIT)


def _layer_spec(l, shape):
    nd = len(shape)
    return pl.BlockSpec((None,) + shape, lambda *_: (l,) + (0,) * nd, pipeline_mode=pl.Buffered(1))


def _mod_spec(l):
    return pl.BlockSpec((None, 1, SUBLANES, D_MODEL), lambda b, t: (l, b, 0, 0))


def _mod_kernel(ct_ref, w_ref, b_ref, o_ref, *, bsz):
    ct = ct_ref[...]
    cond = ct * _sigmoid(ct)
    w = w_ref[0]
    row = lax.broadcasted_iota(jnp.int32, (SUBLANES, w.shape[1]), 0)
    out = jnp.zeros((SUBLANES, w.shape[1]), F32)
    for b in range(bsz):
        out = jnp.where(row == b, jnp.sum(cond[:, b:b + 1] * w, axis=0, keepdims=True) + b_ref[0], out)
    o_ref[0] = out


def _modulation(c, ada_w, ada_b):
    bsz = c.shape[0]
    ct = jnp.pad(c, ((0, SUBLANES - bsz), (0, 0))).T
    tn = 1536
    n = 6 * D_MODEL
    out = pl.pallas_call(
        functools.partial(_mod_kernel, bsz=bsz),
        grid=(DEPTH, n // tn),
        in_specs=[
            pl.BlockSpec((D_MODEL, SUBLANES), lambda l, j: (0, 0)),
            pl.BlockSpec((1, D_MODEL, tn), lambda l, j: (l, 0, j)),
            pl.BlockSpec((1, 1, tn), lambda l, j: (l, 0, j)),
        ],
        out_specs=pl.BlockSpec((1, SUBLANES, tn), lambda l, j: (l, 0, j)),
        out_shape=jax.ShapeDtypeStruct((DEPTH, SUBLANES, n), F32),
        compiler_params=_params("arbitrary", "arbitrary"),
        name="adaln_modulation",
    )(ct, ada_w, ada_b.reshape(DEPTH, 1, n))
    mod = out[:, :bsz].reshape(DEPTH, bsz, 6, D_MODEL)
    return jnp.pad(mod, ((0, 0), (0, 0), (0, SUBLANES - 6), (0, 0)))


def _mod_rmsnorm(x, g, shift, scale):
    ms = jnp.mean(x * x, axis=-1, keepdims=True)
    return (x * lax.rsqrt(ms + EPS) * g) * (1.0 + scale) + shift


def _inproj_kernel(x_ref, mod_ref, g_ref, w_ref, gq_ref, gk_ref, a_ref, zg_ref, qa_ref, ka_ref, vt_ref, km_s):
    t = pl.program_id(1)

    @pl.when(t == 0)
    def _():
        km_s[...] = jnp.zeros(km_s.shape, F32)

    mod = mod_ref[0]
    h = _mod_rmsnorm(x_ref[0], g_ref[...], mod[0:1], mod[1:2]).astype(BF16)
    tn = 512

    def proj(col):
        return jnp.dot(h, w_ref[:, col:col + tn], preferred_element_type=F32)

    zq, zk, zv = proj(Q_COL), proj(K_COL), proj(V_COL)

    def glu():
        a_ref[0] = (proj(0) * _sigmoid(proj(CONV_CH))).astype(BF16)

    def gate_cols(n0):
        zg_ref[0, :, n0:n0 + tn] = proj(GATE_COL + n0).astype(BF16)

    rest = [glu] + [functools.partial(gate_cols, n0) for n0 in range(0, 2 * D_MODEL, tn)]
    lane = lax.broadcasted_iota(jnp.int32, (MOBA_BLOCK, LANES), 1)
    low = lane < HEAD_DIM

    def head_norm(tv, g):
        t2 = tv * tv
        s0 = jnp.sum(jnp.where(low, t2, 0.0), axis=-1, keepdims=True)
        s1 = jnp.sum(jnp.where(low, 0.0, t2), axis=-1, keepdims=True)
        ms = jnp.where(low, s0, s1) * (1.0 / HEAD_DIM)
        return tv * lax.rsqrt(ms + EPS) * g

    blocks_per_tile = TM // MOBA_BLOCK
    nblk = km_s.shape[1]
    ones_rows = jnp.where(lax.broadcasted_iota(jnp.int32, (V_ROWS - HEAD_DIM, MOBA_BLOCK), 0) == 0, 1.0, 0.0)
    feat = lax.broadcasted_iota(jnp.int32, (LANES, MOBA_BLOCK), 0)
    blk = lax.broadcasted_iota(jnp.int32, (nblk, MOBA_BLOCK), 0)
    for sb in range(blocks_per_tile):
        r0 = sb * MOBA_BLOCK
        own = t * blocks_per_tile + sb
        onehot = jnp.where(lane - HEAD_DIM == own, 1.0, 0.0)
        past = blk < own
        for p in range(HEAD_PAIRS):
            if rest:
                rest.pop(0)()
            c0 = p * LANES
            kn = head_norm(zk[r0:r0 + MOBA_BLOCK, c0:c0 + LANES], gk_ref[...])
            km_s[p, pl.ds(own, 1), :] = jnp.mean(kn, axis=0, keepdims=True)
            qt = head_norm(zq[r0:r0 + MOBA_BLOCK, c0:c0 + LANES], gq_ref[...]).T
            vt = zv[r0:r0 + MOBA_BLOCK, c0:c0 + LANES].T
            for hh, kh in enumerate((kn, pltpu.roll(kn, HEAD_DIM, 1))):
                ka_ref[0, 2 * p + hh, r0:r0 + MOBA_BLOCK, :] = jnp.where(low, kh, onehot).astype(BF16)
                vh = vt[hh * HEAD_DIM:(hh + 1) * HEAD_DIM]
                vt_ref[0, 2 * p + hh, sb] = jnp.concatenate([vh, ones_rows], axis=0).astype(BF16)
                hmask = (feat >= hh * HEAD_DIM) & (feat < (hh + 1) * HEAD_DIM)
                gate = jnp.dot(km_s[p], jnp.where(hmask, qt, 0.0), precision=lax.Precision.HIGHEST,
                               preferred_element_type=F32)
                g = jnp.where(past, gate, -jnp.inf)
                sel = blk == own
                for _ in range(MOBA_TOPK):
                    mx = jnp.max(g, axis=0, keepdims=True)
                    idx = jnp.min(jnp.where(g == mx, blk, nblk), axis=0, keepdims=True)
                    hit = blk == idx
                    sel = sel | (hit & past)
                    g = jnp.where(hit, -jnp.inf, g)
                qa_ref[0, 2 * p + hh, sb] = jnp.concatenate(
                    [qt[hh * HEAD_DIM:(hh + 1) * HEAD_DIM] * (QK_SCALE * LOG2E), jnp.where(sel, 0.0, NEG),
                     jnp.zeros((LANES - HEAD_DIM - nblk, MOBA_BLOCK), F32)], axis=0).astype(BF16)
    for task in rest:
        task()


def _inproj(l, x, mod, g, w, gq2, gk2):
    bsz, seq, _ = x.shape
    nb = seq // MOBA_BLOCK
    bpt = TM // MOBA_BLOCK
    tok = lambda b, t: (b, t, 0)
    return pl.pallas_call(
        _inproj_kernel,
        grid=(bsz, seq // TM),
        in_specs=[
            pl.BlockSpec((1, TM, D_MODEL), tok),
            _mod_spec(l),
            _layer_spec(l, (1, D_MODEL)),
            _layer_spec(l, (D_MODEL, IN_WIDTH)),
            _layer_spec(l, (1, LANES)),
            _layer_spec(l, (1, LANES)),
        ],
        out_specs=[
            pl.BlockSpec((1, TM, CONV_CH), tok),
            pl.BlockSpec((1, TM, 2 * D_MODEL), tok),
            pl.BlockSpec((1, N_HEADS, bpt, LANES, MOBA_BLOCK), lambda b, t: (b, 0, t, 0, 0)),
            pl.BlockSpec((1, N_HEADS, TM, LANES), lambda b, t: (b, 0, t, 0)),
            pl.BlockSpec((1, N_HEADS, bpt, V_ROWS, MOBA_BLOCK), lambda b, t: (b, 0, t, 0, 0)),
        ],
        out_shape=[
            jax.ShapeDtypeStruct((bsz, seq, CONV_CH), BF16),
            jax.ShapeDtypeStruct((bsz, seq, 2 * D_MODEL), BF16),
            jax.ShapeDtypeStruct((bsz, N_HEADS, nb, LANES, MOBA_BLOCK), BF16),
            jax.ShapeDtypeStruct((bsz, N_HEADS, seq, LANES), BF16),
            jax.ShapeDtypeStruct((bsz, N_HEADS, nb, V_ROWS, MOBA_BLOCK), BF16),
        ],
        scratch_shapes=[pltpu.VMEM((HEAD_PAIRS, nb, LANES), F32)],
        compiler_params=_params("arbitrary", "arbitrary"),
        name="norm_inproj",
    )(x, mod, g, w, gq2, gk2)


BIAS_STRIP = 32


def _bias_tile_kernel(rb_ref, o_ref):
    dblk = pl.program_id(0)
    max_exact = NUM_BUCKETS // 2
    for k0 in range(0, MOBA_BLOCK, BIAS_STRIP):
        k = k0 + lax.broadcasted_iota(jnp.int32, (BIAS_STRIP, MOBA_BLOCK), 0)
        q = lax.broadcasted_iota(jnp.int32, (BIAS_STRIP, MOBA_BLOCK), 1)
        d = dblk * MOBA_BLOCK + q - k
        dist = jnp.maximum(d, 0)
        nf = jnp.maximum(dist, 1).astype(F32)
        large = max_exact + (jnp.log(nf / max_exact) / math.log(MAX_DISTANCE / max_exact)
                             * (NUM_BUCKETS - max_exact)).astype(jnp.int32)
        large = jnp.minimum(large, NUM_BUCKETS - 1)
        bucket = jnp.where(dist < max_exact, dist, large)
        for h in range(N_HEADS):
            val = jnp.zeros((BIAS_STRIP, MOBA_BLOCK), F32)
            for b in range(NUM_BUCKETS):
                val = jnp.where(bucket == b, rb_ref[h, b], val)
            o_ref[h, 0, k0:k0 + BIAS_STRIP, :] = jnp.where(d >= 0, val * LOG2E, NEG)


def _bias_tiles(rel_bias):
    return pl.pallas_call(
        _bias_tile_kernel,
        grid=(N_BIAS_TILES,),
        in_specs=[pl.BlockSpec(memory_space=pltpu.SMEM)],
        out_specs=pl.BlockSpec((N_HEADS, 1, MOBA_BLOCK, MOBA_BLOCK), lambda d: (0, d, 0, 0)),
        out_shape=jax.ShapeDtypeStruct((N_HEADS, N_BIAS_TILES, MOBA_BLOCK, MOBA_BLOCK), F32),
        compiler_params=_params("arbitrary"),
        name="t5_bias_tiles",
    )(rel_bias)


def _key_groups(i):
    n_far = max(i - (FAR_BLOCKS - 1), 0)
    groups = []
    for lo, hi, near in ((0, n_far, False), (n_far, i + 1, True)):
        parts = -(-(hi - lo) // KB_GROUP)
        for k in range(parts):
            b0 = lo + (hi - lo) * k // parts
            b1 = lo + (hi - lo) * (k + 1) // parts
            groups.append((b0, b1 - b0, near))
    return groups


def _attn_kernel(qa_ref, ka_ref, vt_ref, bias_ref, o_ref, *, first_q):
    groups = [_key_groups(first_q + u) for u in range(Q_CLASS)]
    ahead = [None] * Q_CLASS
    carries = [None] * Q_CLASS
    for g in range(-1, max(len(gs) for gs in groups)):
        for u in range(Q_CLASS):
            if g >= len(groups[u]):
                continue
            q_augs = [qa_ref[0, hh, u] for hh in range(2)]
            scores, softmax_pv = _stream_ops(ka_ref, vt_ref, bias_ref, q_augs, first_q + u)
            if g < 0:
                carries[u] = tuple((jnp.full((1, MOBA_BLOCK), -jnp.inf, F32), jnp.zeros((V_ROWS, MOBA_BLOCK), F32))
                                   for _ in range(2))
                ahead[u] = scores(groups[u][0])
                continue
            ss = ahead[u]
            ahead[u] = scores(groups[u][g + 1]) if g + 1 < len(groups[u]) else None
            carries[u] = softmax_pv(ss, groups[u][g], carries[u])
    for u in range(Q_CLASS):
        outs = [acc[:HEAD_DIM] * (1.0 / acc[HEAD_DIM:HEAD_DIM + 1]) for _, acc in carries[u]]
        o_ref[0, u * MOBA_BLOCK:(u + 1) * MOBA_BLOCK, :] = jnp.concatenate(outs, axis=0).T.astype(BF16)


def _stream_ops(ka_ref, vt_ref, bias_ref, q_augs, i):
    def scores(group):
        b0, n, _ = group
        rows = [slice((b0 + u) * MOBA_BLOCK, (b0 + u + 1) * MOBA_BLOCK) for u in range(n)]
        return [[jnp.dot(ka_ref[0, hh, r, :], q_augs[hh], preferred_element_type=F32) for r in rows]
                for hh in range(2)]

    def softmax_pv(ss, group, carry):
        b0, n, near = group
        blocks = [b0 + u for u in range(n)]
        ps, stats = [], []
        for hh in range(2):
            m, _ = carry[hh]
            if near:
                s = [sv + bias_ref[hh, min(i - b, FAR_BLOCKS)] for sv, b in zip(ss[hh], blocks)]
                shift_by = 0.0
            else:
                s = ss[hh]
                shift_by = bias_ref[hh, FAR_BLOCKS, 0:1, 0:1]
            mx = jnp.max(functools.reduce(jnp.maximum, s), axis=0, keepdims=True)
            m_new = jnp.maximum(m, mx + shift_by)
            shift = m_new - shift_by
            alpha = jnp.exp2(m - m_new)
            stats.append((m_new, alpha))
            ps.append(jnp.concatenate([jnp.exp2(sv - shift).astype(BF16) for sv in s], axis=0))
        out = []
        for hh in range(2):
            m_new, alpha = stats[hh]
            vj = jnp.concatenate([vt_ref[0, hh, b] for b in blocks], axis=1)
            acc = alpha * carry[hh][1] + jnp.dot(vj, ps[hh], preferred_element_type=F32)
            out.append((m_new, acc))
        return tuple(out)

    return scores, softmax_pv


def _attn_instance(qa_ref, ka_ref, vt_ref, bias_ref, prev_ref, o_ref, *, first_q):
    del prev_ref
    _attn_kernel(qa_ref, ka_ref, vt_ref, bias_ref, o_ref, first_q=first_q)


def _attention(qa, ka, vt, bias):
    bsz, _, seq, _ = ka.shape
    nb = seq // MOBA_BLOCK
    out = jnp.zeros((bsz, seq, ATTN_WIDTH), BF16)
    for first in range(0, nb, Q_CLASS):
        keys = first + Q_CLASS
        out = pl.pallas_call(
            functools.partial(_attn_instance, first_q=first),
            grid=(bsz, HEAD_PAIRS),
            in_specs=[
                pl.BlockSpec((1, 2, Q_CLASS, LANES, MOBA_BLOCK),
                             lambda b, p, first=first: (b, p, first // Q_CLASS, 0, 0)),
                pl.BlockSpec((1, 2, keys * MOBA_BLOCK, LANES), lambda b, p: (b, p, 0, 0)),
                pl.BlockSpec((1, 2, keys, V_ROWS, MOBA_BLOCK), lambda b, p: (b, p, 0, 0, 0)),
                pl.BlockSpec((2, N_BIAS_TILES, MOBA_BLOCK, MOBA_BLOCK), lambda b, p: (p, 0, 0, 0)),
                pl.BlockSpec(memory_space=pl.ANY),
            ],
            out_specs=pl.BlockSpec((1, Q_CLASS * MOBA_BLOCK, LANES),
                                   lambda b, p, first=first: (b, first // Q_CLASS, p)),
            out_shape=jax.ShapeDtypeStruct((bsz, seq, ATTN_WIDTH), BF16),
            input_output_aliases={4: 0},
            compiler_params=_params("parallel", "arbitrary"),
            name=f"moba_attention_q{first}",
        )(qa, ka, vt, bias, out)
    return out


def _mixout_kernel(a_ref, halo_ref, zg_ref, o_ref, x_ref, mod_ref, cw_ref, cb_ref, lg_ref, lb_ref,
                   wpa_ref, wpb_ref, wo_ref, out_ref, a_ext, a_sh, y_buf):
    t = pl.program_id(1)
    a_ext[CONV_HALO:, :] = a_ref[0].astype(F32)
    a_ext[:CONV_HALO, :] = jnp.where(t > 0, halo_ref[0].astype(F32), 0.0)
    cw = cw_ref[...]
    cb = cb_ref[...]
    first = CONV_HALO - (CONV_KERNEL - 1)
    n_sh = a_sh.shape[1]
    for s in range(1, SUBLANES):
        a_sh[s - 1] = a_ext[s:s + n_sh, :]
    for r0 in range(0, TM, CONV_ROWS):
        acc = jnp.broadcast_to(cb, (CONV_ROWS, CONV_CH))
        for j in range(CONV_KERNEL):
            q, s = divmod(first + j, SUBLANES)
            lo = r0 + SUBLANES * q
            tap = a_ext[lo:lo + CONV_ROWS, :] if s == 0 else a_sh[s - 1, lo:lo + CONV_ROWS, :]
            acc = acc + cw[j:j + 1] * tap
        y_buf[r0:r0 + CONV_ROWS, :] = acc
    y = y_buf[...]
    mu = jnp.mean(y, axis=-1, keepdims=True)
    yc = y - mu
    var = jnp.mean(yc * yc, axis=-1, keepdims=True)
    yn = yc * lax.rsqrt(var + EPS) * lg_ref[...] + lb_ref[...]
    act = (yn * _sigmoid(yn)).astype(BF16)
    ya = jnp.dot(act, wpa_ref[...], preferred_element_type=F32)
    yb = jnp.dot(o_ref[0], wpb_ref[...], preferred_element_type=F32)
    ga = zg_ref[0, :, :D_MODEL].astype(F32)
    gb = zg_ref[0, :, D_MODEL:].astype(F32)
    m = (_sigmoid(ga) * ya + _sigmoid(gb) * yb).astype(BF16)
    y = jnp.dot(m, wo_ref[...], preferred_element_type=F32)
    out_ref[0] = x_ref[0] + mod_ref[0][2:3] * y


def _mixout(l, a, zg, o, x, mod, cw, cb, lg, lb, wpa, wpb, wo):
    bsz, seq, _ = x.shape
    per = TM // CONV_HALO
    tok = lambda b, t: (b, t, 0)
    return pl.pallas_call(
        _mixout_kernel,
        grid=(bsz, seq // TM),
        in_specs=[
            pl.BlockSpec((1, TM, CONV_CH), tok),
            pl.BlockSpec((1, CONV_HALO, CONV_CH), lambda b, t: (b, jnp.maximum(t * per - 1, 0), 0)),
            pl.BlockSpec((1, TM, 2 * D_MODEL), tok),
            pl.BlockSpec((1, TM, ATTN_WIDTH), tok),
            pl.BlockSpec((1, TM, D_MODEL), tok),
            _mod_spec(l),
            _layer_spec(l, (CONV_HALO, CONV_CH)),
            _layer_spec(l, (1, CONV_CH)),
            _layer_spec(l, (1, CONV_CH)),
            _layer_spec(l, (1, CONV_CH)),
            _layer_spec(l, (CONV_CH, D_MODEL)),
            _layer_spec(l, (ATTN_WIDTH, D_MODEL)),
            _layer_spec(l, (D_MODEL, D_MODEL)),
        ],
        out_specs=pl.BlockSpec((1, TM, D_MODEL), tok),
        out_shape=jax.ShapeDtypeStruct((bsz, seq, D_MODEL), F32),
        scratch_shapes=[pltpu.VMEM((TM + CONV_HALO, CONV_CH), F32),
                        pltpu.VMEM((SUBLANES - 1, TM + CONV_HALO - SUBLANES, CONV_CH), F32),
                        pltpu.VMEM((TM, CONV_CH), F32)],
        compiler_params=_params("parallel", "arbitrary"),
        name="mixer_out",
    )(a, a, zg, o, x, mod, cw, cb, lg, lb, wpa, wpb, wo)


def _ffn_kernel(x_ref, mod_ref, g_ref, wup_ref, fcw_ref, fcb_ref, wdn_ref, out_ref, act_buf, u_tail):
    @pl.when(pl.program_id(1) == 0)
    def _():
        u_tail[...] = jnp.zeros(u_tail.shape, F32)

    mod = mod_ref[0]
    x = x_ref[0]
    h = _mod_rmsnorm(x, g_ref[...], mod[3:4], mod[4:5]).astype(BF16)

    def conv(col):
        u = jnp.dot(h, wup_ref[:, col:col + FF_CHUNK], preferred_element_type=F32)
        ue = jnp.concatenate([u_tail[:, col:col + FF_CHUNK], u], axis=0)
        u_tail[:, col:col + FF_CHUNK] = u[TM - FFN_HALO:]
        w = fcw_ref[:, col:col + FF_CHUNK]
        y = w[2:3] * ue + w[1:2] * pltpu.roll(ue, 1, 0) + w[0:1] * pltpu.roll(ue, 2, 0)
        return y[FFN_HALO:] + fcb_ref[:, col:col + FF_CHUNK]

    def down(g0, g1):
        return jnp.dot(act_buf[:, g0:g1], wdn_ref[g0:g1, :], preferred_element_type=F32)

    acc = None
    pending = None
    assert sum(FF_GROUPS) * FF_CHUNK == D_FF
    bounds = [FF_CHUNK * sum(FF_GROUPS[:k]) for k in range(len(FF_GROUPS) + 1)]
    for g0, g1 in zip(bounds[:-1], bounds[1:]):
        for c0 in range(g0, g1, FF_CHUNK):
            uv = conv(c0)
            ug = conv(D_FF + c0)
            act_buf[:, c0:c0 + FF_CHUNK] = (ug * _sigmoid(ug) * uv).astype(BF16)
            if pending is not None:
                part = down(*pending)
                acc = part if acc is None else acc + part
                pending = None
        pending = (g0, g1)
    part = down(*pending)
    acc = part if acc is None else acc + part
    out_ref[0] = x + mod[5:6] * acc


def _ffn(l, x, mod, g, wup, fcw, fcb, wdn):
    bsz, seq, _ = x.shape
    return pl.pallas_call(
        _ffn_kernel,
        grid=(bsz, seq // TM),
        in_specs=[
            pl.BlockSpec((1, TM, D_MODEL), lambda b, t: (b, t, 0)),
            _mod_spec(l),
            _layer_spec(l, (1, D_MODEL)),
            _layer_spec(l, (D_MODEL, 2 * D_FF)),
            _layer_spec(l, (SUBLANES, 2 * D_FF)),
            _layer_spec(l, (1, 2 * D_FF)),
            _layer_spec(l, (D_FF, D_MODEL)),
        ],
        out_specs=pl.BlockSpec((1, TM, D_MODEL), lambda b, t: (b, t, 0)),
        out_shape=jax.ShapeDtypeStruct((bsz, seq, D_MODEL), F32),
        scratch_shapes=[pltpu.VMEM((TM, D_FF), BF16), pltpu.VMEM((FFN_HALO, 2 * D_FF), F32)],
        compiler_params=_params("arbitrary", "arbitrary"),
        name="conv_ffn",
    )(x, mod, g, wup, fcw, fcb, wdn)


def kernel(x, c, ada_w, ada_b, norm1_g, w_in, conv_w, conv_b, conv_ln_g, conv_ln_b, q_norm_g, k_norm_g,
           w_pa, w_pb, w_o, norm2_g, w_up, ffn_conv_w, ffn_conv_b, w_down, rel_bias):
    mod = _modulation(c, ada_w, ada_b)
    bias = _bias_tiles(rel_bias)
    row = lambda p: p[:, None, :]
    cw = jnp.pad(conv_w, ((0, 0), (0, CONV_HALO - CONV_KERNEL), (0, 0)))
    fcw = jnp.pad(ffn_conv_w, ((0, 0), (0, SUBLANES - FFN_CONV), (0, 0)))
    gq2, gk2 = row(jnp.tile(q_norm_g, (1, 2))), row(jnp.tile(k_norm_g, (1, 2)))
    w_in, w_pa, w_pb, w_o, w_up, w_down = (w.astype(BF16) for w in (w_in, w_pa, w_pb, w_o, w_up, w_down))
    for l in range(DEPTH):
        a, zg, qa, ka, vt = _inproj(l, x, mod, row(norm1_g), w_in, gq2, gk2)
        o = _attention(qa, ka, vt, bias)
        x = _mixout(l, a, zg, o, x, mod, cw, row(conv_b), row(conv_ln_g), row(conv_ln_b), w_pa, w_pb, w_o)
        x = _ffn(l, x, mod, row(norm2_g), w_up, fcw, row(ffn_conv_b), w_down)
    return x
```
